```python
import math
import jax
import jax.numpy as jnp
from jax import lax
import numpy as np

D_MODEL = 1024
BATCH = 4
SEQ = 4096
DEPTH = 2
DEC_BATCH = 128
DEC_SEQ = 1
PAST_LEN = 16384
PAGE_SIZE = 128

N_HEADS = 16
N_KV_HEADS = 4
HEAD_DIM = 64
GROUP = N_HEADS // N_KV_HEADS
WINDOW = 128
BAND_BLOCK = 128
MOBA_BLOCK = 256
MOBA_TOPK = 3
MOBA_Q_BLOCK = 128
N_BUCKETS = 32
REL_MAX_DIST = 1024
D_FF = 3584
N_EXPERTS = 8
EXPERT_TOPK = 2
N_A_LAYERS = DEPTH // 2
N_B_LAYERS = DEPTH - N_A_LAYERS
N_DENSE = (DEPTH + 1) // 2
N_MOE = DEPTH // 2
RMS_EPS = 1e-5
NEG_INF = -1e30
QKV_WIDTH = (N_HEADS + 2 * N_KV_HEADS) * HEAD_DIM
KV_WIDTH = 2 * N_KV_HEADS * HEAD_DIM
ATTN_SCALE = HEAD_DIM ** -0.5

kernel_name = "yoco_swa_sink_moba_hybrid_step"


def rms_norm(x, g):
    xf = x.astype(jnp.float32)
    y = xf * lax.rsqrt(jnp.mean(xf * xf, axis=-1, keepdims=True) + RMS_EPS)
    return (y * g.astype(jnp.float32)).astype(x.dtype)


def rel_bucket(dist):
    max_exact = N_BUCKETS // 2
    n = jnp.maximum(dist, 0)
    nf = jnp.maximum(n, 1).astype(jnp.float32)
    large = max_exact + (jnp.log(nf / max_exact) / math.log(REL_MAX_DIST / max_exact)
                         * (N_BUCKETS - max_exact)).astype(jnp.int32)
    return jnp.where(n < max_exact, n, jnp.minimum(large, N_BUCKETS - 1))


def sink_softmax(logits, sink):
    sink = sink.astype(jnp.float32)
    m = jnp.maximum(jnp.max(logits, axis=-1, keepdims=True), sink)
    e = jnp.exp(logits - m)
    return e / (jnp.sum(e, axis=-1, keepdims=True) + jnp.exp(sink - m))


def split_qkv(z):
    lead = z.shape[:-1]
    hq = N_HEADS * HEAD_DIM
    hk = N_KV_HEADS * HEAD_DIM
    q = z[..., :hq].reshape(*lead, N_HEADS, HEAD_DIM)
    k = z[..., hq:hq + hk].reshape(*lead, N_KV_HEADS, HEAD_DIM)
    v = z[..., hq + hk:].reshape(*lead, N_KV_HEADS, HEAD_DIM)
    return q, k, v


def shared_kv(y, g, w):
    z = rms_norm(y, g) @ w
    lead = z.shape[:-1]
    hk = N_KV_HEADS * HEAD_DIM
    return (z[..., :hk].reshape(*lead, N_KV_HEADS, HEAD_DIM),
            z[..., hk:].reshape(*lead, N_KV_HEADS, HEAD_DIM))


def swiglu(x, w_gate, w_up, w_down):
    return (jax.nn.silu(x @ w_gate) * (x @ w_up)) @ w_down


def moe_swiglu(x, w_router, w_gate, w_up, w_down):
    logits = (x @ w_router).astype(jnp.float32)
    top_val, top_idx = lax.top_k(logits, EXPERT_TOPK)
    gates = jax.nn.softmax(top_val, axis=-1)
    gate_full = jnp.sum(jax.nn.one_hot(top_idx, N_EXPERTS, dtype=jnp.float32) * gates[..., None], axis=-2)
    out = jnp.zeros(x.shape, jnp.float32)
    for e in range(N_EXPERTS):
        out = out + gate_full[..., e:e + 1] * swiglu(x, w_gate[e], w_up[e], w_down[e]).astype(jnp.float32)
    return out.astype(x.dtype)


def window_attn_prompt(q, k, v, sink, table):
    B, S = q.shape[0], q.shape[1]
    nb = S // BAND_BLOCK
    qb = q.reshape(B, nb, BAND_BLOCK, N_KV_HEADS, GROUP, HEAD_DIM)

    def with_prev(t):
        t = t.reshape(B, nb, BAND_BLOCK, N_KV_HEADS, HEAD_DIM)
        prev = jnp.concatenate([jnp.zeros_like(t[:, :1]), t[:, :-1]], axis=1)
        return jnp.concatenate([prev, t], axis=2)

    k2, v2 = with_prev(k), with_prev(v)
    i = jnp.arange(BAND_BLOCK)[:, None]
    j = jnp.arange(2 * BAND_BLOCK)[None, :]
    dist = BAND_BLOCK + i - j
    band = (dist >= 0) & (dist <= WINDOW)
    blk = jnp.arange(nb)[:, None, None]
    valid = band[None] & ((blk > 0) | (j[None] >= BAND_BLOCK))
    bias = table[rel_bucket(dist)].astype(jnp.float32)
    bias = bias.reshape(BAND_BLOCK, 2 * BAND_BLOCK, N_KV_HEADS, GROUP).transpose(2, 3, 0, 1)
    logits = jnp.einsum("bnqhgd,bnkhd->bnhgqk", qb, k2).astype(jnp.float32) * ATTN_SCALE + bias
    logits = jnp.where(valid[None, :, None, None], logits, NEG_INF)
    p = sink_softmax(logits, sink.reshape(N_KV_HEADS, GROUP, 1, 1))
    out = jnp.einsum("bnhgqk,bnkhd->bnqhgd", p.astype(v.dtype), v2)
    return out.reshape(B, S, N_HEADS * HEAD_DIM)


def window_attn_sample(q, k_new, v_new, k_buf, v_buf, sink, table):
    DB, DS = q.shape[0], q.shape[1]
    W = k_buf.shape[1]
    k = jnp.concatenate([k_buf, k_new], axis=1)
    v = jnp.concatenate([v_buf, v_new], axis=1)
    q_pos = PAST_LEN + jnp.arange(DS)
    k_pos = PAST_LEN - W + jnp.arange(W + DS)
    dist = q_pos[:, None] - k_pos[None, :]
    valid = (dist >= 0) & (dist <= WINDOW)
    bias = table[rel_bucket(dist)].astype(jnp.float32)
    bias = bias.reshape(DS, W + DS, N_KV_HEADS, GROUP).transpose(2, 3, 0, 1)
    qg = q.reshape(DB, DS, N_KV_HEADS, GROUP, HEAD_DIM)
    logits = jnp.einsum("bqhgd,bkhd->bhgqk", qg, k).astype(jnp.float32) * ATTN_SCALE + bias
    logits = jnp.where(valid, logits, NEG_INF)
    p = sink_softmax(logits, sink.reshape(N_KV_HEADS, GROUP, 1, 1))
    out = jnp.einsum("bhgqk,bkhd->bqhgd", p.astype(v.dtype), v).reshape(DB, DS, N_HEADS * HEAD_DIM)
    return out, k[:, DS:], v[:, DS:]


def moba_prompt(q, k, v, table):
    B, S = q.shape[0], q.shape[1]
    n_blk = -(-S // MOBA_BLOCK)
    pad = n_blk * MOBA_BLOCK - S
    kvh = jnp.arange(N_HEADS) // GROUP
    hidx = jnp.arange(N_HEADS)

    def to_blocks(t):
        t = jnp.pad(t, ((0, 0), (0, pad), (0, 0), (0, 0)))
        return t.reshape(B, n_blk, MOBA_BLOCK, N_KV_HEADS, HEAD_DIM).transpose(0, 3, 1, 2, 4)

    kb, vb = to_blocks(k), to_blocks(v)
    k_mean = jnp.mean(kb.astype(jnp.float32), axis=3)
    gate = jnp.einsum("bshd,bhnd->bshn", q.astype(jnp.float32), k_mean[:, kvh])
    q_blk = jnp.arange(S) // MOBA_BLOCK
    past = jnp.arange(n_blk)[None, :] < q_blk[:, None]
    gate = jnp.where(past[None, :, None, :], gate, NEG_INF)
    n_sel = min(MOBA_TOPK, n_blk)
    _, sel = lax.top_k(gate, n_sel)
    sel_ok = sel < q_blk[None, :, None, None]
    n_qc = S // MOBA_Q_BLOCK
    n_items = B * n_qc
    n_keys = n_sel * MOBA_BLOCK
    xs = (q.reshape(n_items, MOBA_Q_BLOCK, N_HEADS, HEAD_DIM),
          sel.reshape(n_items, MOBA_Q_BLOCK, N_HEADS, n_sel),
          sel_ok.reshape(n_items, MOBA_Q_BLOCK, N_HEADS, n_sel),
          jnp.repeat(jnp.arange(B), n_qc),
          jnp.tile(jnp.arange(n_qc), B))

    def step(item):
        q_i, sel_i, ok_i, b, c = item
        kb_b, vb_b = kb[b], vb[b]
        q_pos = c * MOBA_Q_BLOCK + jnp.arange(MOBA_Q_BLOCK)
        k_sel = kb_b[kvh[None, :, None], sel_i]
        v_sel = vb_b[kvh[None, :, None], sel_i]
        k_pos = sel_i[..., None] * MOBA_BLOCK + jnp.arange(MOBA_BLOCK)
        d_sel = q_pos[:, None, None, None] - k_pos
        l_sel = (jnp.einsum("qhd,qhnsd->qhns", q_i, k_sel).astype(jnp.float32) * ATTN_SCALE
                 + table[rel_bucket(d_sel), hidx[None, :, None, None]].astype(jnp.float32))
        l_sel = jnp.where(ok_i[..., None], l_sel, NEG_INF).reshape(MOBA_Q_BLOCK, N_HEADS, n_keys)
        j = (c * MOBA_Q_BLOCK) // MOBA_BLOCK
        k_own = lax.dynamic_index_in_dim(kb_b, j, axis=1, keepdims=False)[kvh]
        v_own = lax.dynamic_index_in_dim(vb_b, j, axis=1, keepdims=False)[kvh]
        d_own = q_pos[:, None] - (j * MOBA_BLOCK + jnp.arange(MOBA_BLOCK))[None, :]
        l_own = (jnp.einsum("qhd,hsd->qhs", q_i, k_own).astype(jnp.float32) * ATTN_SCALE
                 + jnp.moveaxis(table[rel_bucket(d_own)], -1, 1).astype(jnp.float32))
        l_own = jnp.where((d_own >= 0)[:, None, :], l_own, NEG_INF)
        p = jax.nn.softmax(jnp.concatenate([l_sel, l_own], axis=-1), axis=-1).astype(v.dtype)
        return (jnp.einsum("qhn,qhnd->qhd", p[..., :n_keys],
                           v_sel.reshape(MOBA_Q_BLOCK, N_HEADS, n_keys, HEAD_DIM))
                + jnp.einsum("qhs,hsd->qhd", p[..., n_keys:], v_own))

    out = lax.map(step, xs)
    return out.reshape(B, S, N_HEADS * HEAD_DIM)


def moba_sample(q, k_new, v_new, cache_k, cache_v, page_table, table):
    DB, DS = q.shape[0], q.shape[1]
    n_pages = PAST_LEN // PAGE_SIZE
    ppb = MOBA_BLOCK // PAGE_SIZE
    n_full = PAST_LEN // MOBA_BLOCK
    tail_pages = n_pages - n_full * ppb
    kvh = jnp.arange(N_HEADS) // GROUP
    hidx = jnp.arange(N_HEADS)
    q_pos = PAST_LEN + jnp.arange(DS)
    qg = q.reshape(DB, DS, N_KV_HEADS, GROUP, HEAD_DIM)
    tail_pt = page_table[:, n_full * ppb:]
    k_own = jnp.concatenate([cache_k[tail_pt].reshape(DB, tail_pages * PAGE_SIZE, N_KV_HEADS, HEAD_DIM), k_new], axis=1)
    v_own = jnp.concatenate([cache_v[tail_pt].reshape(DB, tail_pages * PAGE_SIZE, N_KV_HEADS, HEAD_DIM), v_new], axis=1)
    l_own_len = k_own.shape[1]
    d_own = q_pos[:, None] - (n_full * MOBA_BLOCK + jnp.arange(l_own_len))[None, :]
    b_own = table[rel_bucket(d_own)].astype(jnp.float32).reshape(DS, l_own_len, N_KV_HEADS, GROUP).transpose(0, 2, 3, 1)
    l_own = jnp.einsum("bqhgd,bkhd->bqhgk", qg, k_own).astype(jnp.float32) * ATTN_SCALE + b_own
    l_own = jnp.where((d_own >= 0)[:, None, None, :], l_own, NEG_INF).reshape(DB, DS, N_HEADS, l_own_len)
    if n_full == 0:
        p_own = jax.nn.softmax(l_own, axis=-1).astype(v_new.dtype)
        out = jnp.einsum("bqhgk,bkhd->bqhgd", p_own.reshape(DB, DS, N_KV_HEADS, GROUP, l_own_len), v_own)
        return out.reshape(DB, DS, N_HEADS * HEAD_DIM)
    n_sel = min(MOBA_TOPK, n_full)
    n_keys = n_sel * MOBA_BLOCK
    page_sum = jnp.sum(cache_k.astype(jnp.float32), axis=1)
    blk_mean = page_sum[page_table[:, :n_full * ppb]].reshape(DB, n_full, ppb, N_KV_HEADS, HEAD_DIM).sum(axis=2) / MOBA_BLOCK
    gate = jnp.einsum("bqhgd,bnhd->bqhgn", qg.astype(jnp.float32), blk_mean).reshape(DB, DS, N_HEADS, n_full)
    _, sel = lax.top_k(gate, n_sel)
    bidx = jnp.arange(DB)[:, None, None, None, None]
    phys = page_table[bidx, sel[..., None] * ppb + jnp.arange(ppb)]
    kv_sel = kvh[None, None, :, None, None]
    k_sel = cache_k[phys, :, kv_sel].reshape(DB, DS, N_HEADS, n_keys, HEAD_DIM)
    v_sel = cache_v[phys, :, kv_sel].reshape(DB, DS, N_HEADS, n_keys, HEAD_DIM)
    k_pos = (sel[..., None] * MOBA_BLOCK + jnp.arange(MOBA_BLOCK)).reshape(DB, DS, N_HEADS, n_keys)
    d_sel = q_pos[None, :, None, None] - k_pos
    l_sel = (jnp.einsum("bqhd,bqhnd->bqhn", q, k_sel).astype(jnp.float32) * ATTN_SCALE
             + table[rel_bucket(d_sel), hidx[None, None, :, None]].astype(jnp.float32))
    p = jax.nn.softmax(jnp.concatenate([l_sel, l_own], axis=-1), axis=-1).astype(v_new.dtype)
    o_sel = jnp.einsum("bqhn,bqhnd->bqhd", p[..., :n_keys], v_sel)
    o_own = jnp.einsum("bqhgk,bkhd->bqhgd", p[..., n_keys:].reshape(DB, DS, N_KV_HEADS, GROUP, l_own_len), v_own)
    return (o_sel + o_own.reshape(DB, DS, N_HEADS, HEAD_DIM)).reshape(DB, DS, N_HEADS * HEAD_DIM)


def setup_inputs(seed: int = 0) -> dict:
    key = jax.random.key(seed)
    ks = jax.random.split(key, 32)
    f32 = jnp.float32
    n_pages = PAST_LEN // PAGE_SIZE
    n_used = DEC_BATCH * n_pages
    n_phys = n_used + max(1, n_used // 4)
    w_buf = min(WINDOW, PAST_LEN)
    hd_all = N_HEADS * HEAD_DIM

    def nrm(k, shape, scale=1.0):
        return jax.random.normal(k, shape, f32) * scale

    page_table = jax.random.permutation(ks[6], n_phys)[:n_used].reshape(DEC_BATCH, n_pages).astype(jnp.int32)
    return {
        "x_prompt": nrm(ks[0], (BATCH, SEQ, D_MODEL)),
        "x_sample": nrm(ks[1], (DEC_BATCH, DEC_SEQ, D_MODEL)),
        "state_win_k": nrm(ks[2], (DEC_BATCH, N_A_LAYERS, w_buf, N_KV_HEADS, HEAD_DIM)),
        "state_win_v": nrm(ks[3], (DEC_BATCH, N_A_LAYERS, w_buf, N_KV_HEADS, HEAD_DIM)),
        "cache_k": nrm(ks[4], (n_phys, PAGE_SIZE, N_KV_HEADS, HEAD_DIM)),
        "cache_v": nrm(ks[5], (n_phys, PAGE_SIZE, N_KV_HEADS, HEAD_DIM)),
        "page_table": page_table,
        "rel_bias_table": nrm(ks[7], (N_BUCKETS, N_HEADS), 0.5),
        "attn_norm_g": 1.0 + nrm(ks[8], (DEPTH, D_MODEL), 0.01),
        "ffn_norm_g": 1.0 + nrm(ks[9], (DEPTH, D_MODEL), 0.01),
        "w_qkv_a": nrm(ks[10], (N_A_LAYERS, D_MODEL, QKV_WIDTH), D_MODEL ** -0.5),
        "sink_a": nrm(ks[11], (N_A_LAYERS, N_HEADS), 1.0),
        "w_o_a": nrm(ks[12], (N_A_LAYERS, hd_all, D_MODEL), hd_all ** -0.5),
        "kv_norm_g": 1.0 + nrm(ks[13], (D_MODEL,), 0.01),
        "w_kv_shared": nrm(ks[14], (D_MODEL, KV_WIDTH), D_MODEL ** -0.5),
        "w_q_b": nrm(ks[15], (N_B_LAYERS, D_MODEL, hd_all), D_MODEL ** -0.5),
        "w_o_b": nrm(ks[16], (N_B_LAYERS, hd_all, D_MODEL), hd_all ** -0.5),
        "w_gate_dense": nrm(ks[17], (N_DENSE, D_MODEL, D_FF), D_MODEL ** -0.5),
        "w_up_dense": nrm(ks[18], (N_DENSE, D_MODEL, D_FF), D_MODEL ** -0.5),
        "w_down_dense": nrm(ks[19], (N_DENSE, D_FF, D_MODEL), D_FF ** -0.5),
        "w_router": nrm(ks[20], (N_MOE, D_MODEL, N_EXPERTS), D_MODEL ** -0.5),
        "w_gate_moe": nrm(ks[21], (N_MOE, N_EXPERTS, D_MODEL, D_FF), D_MODEL ** -0.5),
        "w_up_moe": nrm(ks[22], (N_MOE, N_EXPERTS, D_MODEL, D_FF), D_MODEL ** -0.5),
        "w_down_moe": nrm(ks[23], (N_MOE, N_EXPERTS, D_FF, D_MODEL), D_FF ** -0.5),
        "final_norm_g": 1.0 + nrm(ks[24], (D_MODEL,), 0.01),
    }


def reference(x_prompt, x_sample, state_win_k, state_win_v, cache_k, cache_v, page_table,
              rel_bias_table, attn_norm_g, ffn_norm_g, w_qkv_a, sink_a, w_o_a,
              kv_norm_g, w_kv_shared, w_q_b, w_o_b,
              w_gate_dense, w_up_dense, w_down_dense,
              w_router, w_gate_moe, w_up_moe, w_down_moe, final_norm_g):
    yp, ys = x_prompt, x_sample
    Bp, Sp = yp.shape[0], yp.shape[1]
    Bs, Ss = ys.shape[0], ys.shape[1]
    hd_all = N_HEADS * HEAD_DIM
    win_kp, win_vp, win_ks, win_vs = [], [], [], []
    kv_kp = kv_vp = kv_ks = kv_vs = None
    for layer in range(DEPTH):
        hp = rms_norm(yp, attn_norm_g[layer])
        hs = rms_norm(ys, attn_norm_g[layer])
        if layer < N_A_LAYERS:
            a = layer
            qp, kp, vp = split_qkv(hp @ w_qkv_a[a])
            qs, ks_new, vs_new = split_qkv(hs @ w_qkv_a[a])
            mix_p = window_attn_prompt(qp, kp, vp, sink_a[a], rel_bias_table)
            mix_s, buf_k, buf_v = window_attn_sample(qs, ks_new, vs_new, state_win_k[:, a], state_win_v[:, a],
                                                     sink_a[a], rel_bias_table)
            w_keep = min(WINDOW, Sp)
            win_kp.append(kp[:, Sp - w_keep:])
            win_vp.append(vp[:, Sp - w_keep:])
            win_ks.append(buf_k)
            win_vs.append(buf_v)
            yp = yp + mix_p @ w_o_a[a]
            ys = ys + mix_s @ w_o_a[a]
        else:
            bl = layer - N_A_LAYERS
            if layer == N_A_LAYERS:
                kv_kp, kv_vp = shared_kv(yp, kv_norm_g, w_kv_shared)
                kv_ks, kv_vs = shared_kv(ys, kv_norm_g, w_kv_shared)
            qp = (hp @ w_q_b[bl]).reshape(Bp, Sp, N_HEADS, HEAD_DIM)
            qs = (hs @ w_q_b[bl]).reshape(Bs, Ss, N_HEADS, HEAD_DIM)
            mix_p = moba_prompt(qp, kv_kp, kv_vp, rel_bias_table)
            mix_s = moba_sample(qs, kv_ks, kv_vs, cache_k, cache_v, page_table, rel_bias_table)
            yp = yp + mix_p @ w_o_b[bl]
            ys = ys + mix_s @ w_o_b[bl]
        gp = rms_norm(yp, ffn_norm_g[layer])
        gs = rms_norm(ys, ffn_norm_g[layer])
        if layer % 2 == 0:
            d = layer // 2
            yp = yp + swiglu(gp, w_gate_dense[d], w_up_dense[d], w_down_dense[d])
            ys = ys + swiglu(gs, w_gate_dense[d], w_up_dense[d], w_down_dense[d])
        else:
            m = layer // 2
            yp = yp + moe_swiglu(gp, w_router[m], w_gate_moe[m], w_up_moe[m], w_down_moe[m])
            ys = ys + moe_swiglu(gs, w_router[m], w_gate_moe[m], w_up_moe[m], w_down_moe[m])
    y_prompt = rms_norm(yp, final_norm_g)
    y_sample = rms_norm(ys, final_norm_g)
    win_k_prompt = jnp.stack(win_kp, axis=1)
    win_v_prompt = jnp.stack(win_vp, axis=1)
    win_k_sample = jnp.stack(win_ks, axis=1)
    win_v_sample = jnp.stack(win_vs, axis=1)
    return (y_prompt, y_sample, win_k_prompt, win_v_prompt, kv_kp, kv_vp, win_k_sample, win_v_sample, kv_ks, kv_vs)
```

```python
import functools
import math

import numpy as np
import jax
import jax.numpy as jnp
from jax import lax
from jax.experimental import pallas as pl
from jax.experimental.pallas import tpu as pltpu

N_HEADS = 16
N_KV_HEADS = 4
HEAD_DIM = 64
GROUP = N_HEADS // N_KV_HEADS
WINDOW = 128
BAND_BLOCK = 128
MOBA_BLOCK = 256
MOBA_TOPK = 3
MOBA_Q_BLOCK = 128
PAGE_SIZE = 128
N_BUCKETS = 32
REL_MAX_DIST = 1024
N_EXPERTS = 8
EXPERT_TOPK = 2
RMS_EPS = 1e-5
NEG_INF = -1e30
ATTN_SCALE = HEAD_DIM ** -0.5
QCOLS = GROUP * MOBA_Q_BLOCK
FAR_DIST = 790
MOBA_FAR_TILE = 9
SAMPLE_FAR_TILE = 4
LANES = 128
KV_PER_CHUNK = LANES // HEAD_DIM

F32 = jnp.float32
BF16 = jnp.bfloat16
VMEM_LIMIT = 56 * 1024 * 1024


def _bucket_np(dist):
    max_exact = N_BUCKETS // 2
    n = np.maximum(np.asarray(dist, np.int64), 0)
    nf = np.maximum(n, 1).astype(np.float64)
    large = max_exact + (np.log(nf / max_exact) / math.log(REL_MAX_DIST / max_exact)
                         * (N_BUCKETS - max_exact)).astype(np.int64)
    return np.where(n < max_exact, n, np.minimum(large, N_BUCKETS - 1)).astype(np.int32)


def _tile(n, pref):
    if n <= pref:
        return n
    for t in range(pref, 7, -8):
        if n % t == 0:
            return t
    return n


def _params(**kw):
    return pltpu.CompilerParams(vmem_limit_bytes=VMEM_LIMIT, **kw)


def _rms(x, g):
    return (x * lax.rsqrt(jnp.mean(x * x, axis=-1, keepdims=True) + RMS_EPS)) * g


def _rms_matmul_kernel(x_ref, g_ref, w_ref, o_ref):
    xn = _rms(x_ref[...], g_ref[...]).astype(BF16)
    o_ref[...] = jnp.dot(xn, w_ref[...], preferred_element_type=F32)


def rms_matmul(x, g, w, tm):
    m, d = x.shape
    n = w.shape[1]
    return pl.pallas_call(
        _rms_matmul_kernel,
        grid=(m // tm,),
        in_specs=[pl.BlockSpec((tm, d), lambda i: (i, 0)),
                  pl.BlockSpec((1, d), lambda i: (0, 0)),
                  pl.BlockSpec((d, n), lambda i: (0, 0))],
        out_specs=pl.BlockSpec((tm, n), lambda i: (i, 0)),
        out_shape=jax.ShapeDtypeStruct((m, n), F32),
        compiler_params=_params(),
        name="rms_matmul",
    )(x, g.reshape(1, d), w)


def _matmul_res_kernel(a_ref, w_ref, r_ref, o_ref):
    o_ref[...] = r_ref[...] + jnp.dot(a_ref[...].astype(BF16), w_ref[...], preferred_element_type=F32)


def matmul_res(a, w, res, tm):
    m, k = a.shape
    n = w.shape[1]
    return pl.pallas_call(
        _matmul_res_kernel,
        grid=(m // tm,),
        in_specs=[pl.BlockSpec((tm, k), lambda i: (i, 0)),
                  pl.BlockSpec((k, n), lambda i: (0, 0)),
                  pl.BlockSpec((tm, n), lambda i: (i, 0))],
        out_specs=pl.BlockSpec((tm, n), lambda i: (i, 0)),
        out_shape=jax.ShapeDtypeStruct((m, n), F32),
        compiler_params=_params(),
        name="matmul_res",
    )(a, w, res)


def _ffn_kernel(x_ref, g_ref, wg_ref, wu_ref, wd_ref, o_ref, xn_ref, acc_ref):
    j = pl.program_id(1)

    @pl.when(j == 0)
    def _():
        xn_ref[...] = _rms(x_ref[...], g_ref[...]).astype(BF16)
        acc_ref[...] = jnp.zeros_like(acc_ref)

    xn = xn_ref[...]
    a = jnp.dot(xn, wg_ref[...], preferred_element_type=F32)
    b = jnp.dot(xn, wu_ref[...], preferred_element_type=F32)
    h = (a * jax.nn.sigmoid(a)) * b
    acc_ref[...] += jnp.dot(h.astype(BF16), wd_ref[...], preferred_element_type=F32)

    @pl.when(j == pl.num_programs(1) - 1)
    def _():
        o_ref[...] = x_ref[...] + acc_ref[...]


def ffn_dense(x, g, wg, wu, wd, tm, tn):
    m, d = x.shape
    f = wg.shape[1]
    return pl.pallas_call(
        _ffn_kernel,
        grid=(m // tm, f // tn),
        in_specs=[pl.BlockSpec((tm, d), lambda i, j: (i, 0)),
                  pl.BlockSpec((1, d), lambda i, j: (0, 0)),
                  pl.BlockSpec((d, tn), lambda i, j: (0, j)),
                  pl.BlockSpec((d, tn), lambda i, j: (0, j)),
                  pl.BlockSpec((tn, d), lambda i, j: (j, 0))],
        out_specs=pl.BlockSpec((tm, d), lambda i, j: (i, 0)),
        out_shape=jax.ShapeDtypeStruct((m, d), F32),
        scratch_shapes=[pltpu.VMEM((tm, d), BF16), pltpu.VMEM((tm, d), F32)],
        compiler_params=_params(),
        name="ffn_dense",
    )(x, g.reshape(1, d), wg, wu, wd)


def _stack_heads(q_ref):
    return jnp.concatenate([q_ref[0, r] for r in range(GROUP)], axis=1)


def _unstack_heads(o_ref, o):
    for r in range(GROUP):
        o_ref[0, r] = o[:, r * MOBA_Q_BLOCK:(r + 1) * MOBA_Q_BLOCK].astype(o_ref.dtype)


def _win_prompt_kernel(q_ref, kp_ref, ko_ref, vp_ref, vo_ref, wb_ref, sink_ref, o_ref):
    n = pl.program_id(1)
    q = (_stack_heads(q_ref) * ATTN_SCALE).astype(BF16)
    k = jnp.concatenate([kp_ref[0, 0], ko_ref[0, 0]], axis=0).astype(BF16)
    s = jnp.dot(k, q, preferred_element_type=F32) + wb_ref[0]
    row = lax.broadcasted_iota(jnp.int32, s.shape, 0)
    s = jnp.where((row >= BAND_BLOCK) | (n > 0), s, NEG_INF)
    sink = sink_ref[0]
    m = jnp.maximum(jnp.max(s, axis=0, keepdims=True), sink)
    e = jnp.exp(s - m)
    den = jnp.sum(e, axis=0, keepdims=True) + jnp.exp(sink - m)
    vt = jnp.concatenate([vp_ref[0, 0], vo_ref[0, 0]], axis=1).astype(BF16)
    o = jnp.dot(vt, e.astype(BF16), preferred_element_type=F32) / den
    _unstack_heads(o_ref, o)


def window_attn_prompt(qt, k4, vt, wb, sink_cols):
    b, _, _, s = qt.shape
    nb = s // BAND_BLOCK
    qb = BAND_BLOCK
    prev = lambda n: jnp.maximum(n - 1, 0)
    return pl.pallas_call(
        _win_prompt_kernel,
        grid=(b, nb, N_KV_HEADS),
        in_specs=[pl.BlockSpec((1, GROUP, HEAD_DIM, qb), lambda b, n, g: (b, g, 0, n)),
                  pl.BlockSpec((1, 1, qb, HEAD_DIM), lambda b, n, g: (b, g, prev(n), 0)),
                  pl.BlockSpec((1, 1, qb, HEAD_DIM), lambda b, n, g: (b, g, n, 0)),
                  pl.BlockSpec((1, 1, HEAD_DIM, qb), lambda b, n, g: (b, g, 0, prev(n))),
                  pl.BlockSpec((1, 1, HEAD_DIM, qb), lambda b, n, g: (b, g, 0, n)),
                  pl.BlockSpec((1, 2 * qb, QCOLS), lambda b, n, g: (g, 0, 0)),
                  pl.BlockSpec((1, 1, QCOLS), lambda b, n, g: (g, 0, 0))],
        out_specs=pl.BlockSpec((1, GROUP, HEAD_DIM, qb), lambda b, n, g: (b, g, 0, n)),
        out_shape=jax.ShapeDtypeStruct(qt.shape, BF16),
        compiler_params=_params(),
        name="window_attn_prompt",
    )(qt, k4, k4, vt, vt, wb, sink_cols)


def _moba_prompt_kernel(q_ref, k_ref, vt_ref, tb_ref, o_ref, kmean_ref, sel_ref, *, n_blk):
    t = pl.program_id(2)
    qb = t // (MOBA_BLOCK // MOBA_Q_BLOCK)

    @pl.when(t == 0)
    def _():
        kf = k_ref[0, 0].reshape(n_blk, MOBA_BLOCK, HEAD_DIM)
        kmean_ref[...] = jnp.mean(kf, axis=1)

    q32 = _stack_heads(q_ref)
    gate = jnp.dot(kmean_ref[...], q32, precision=lax.Precision.HIGHEST,
                   preferred_element_type=F32)
    blk = lax.broadcasted_iota(jnp.int32, gate.shape, 0)
    past = blk < qb
    gate = jnp.where(past, gate, NEG_INF)
    rank = jnp.zeros(gate.shape, jnp.int32)
    for kk in range(n_blk):
        gk = gate[kk:kk + 1, :]
        beats = (gk > gate) | ((gk == gate) & (kk < blk))
        rank = rank + beats.astype(jnp.int32)
    sel = (past & (rank < MOBA_TOPK)) | (blk == qb)
    sel_ref[...] = sel.astype(F32)
    qs = (q32 * ATTN_SCALE).astype(BF16)

    def body(j, carry):
        m, l, acc = carry
        u = jnp.minimum(t - (MOBA_BLOCK // MOBA_Q_BLOCK) * j, MOBA_FAR_TILE)
        off = pl.multiple_of(j * MOBA_BLOCK, MOBA_BLOCK)
        kj = k_ref[0, 0, pl.ds(off, MOBA_BLOCK), :].astype(BF16)
        s = jnp.dot(kj, qs, preferred_element_type=F32) + tb_ref[0, u]
        s = jnp.where(sel_ref[pl.ds(j, 1), :] > 0, s, NEG_INF)
        m_new = jnp.maximum(m, jnp.max(s, axis=0, keepdims=True))
        alpha = jnp.exp(m - m_new)
        p = jnp.exp(s - m_new)
        l = alpha * l + jnp.sum(p, axis=0, keepdims=True)
        vj = vt_ref[0, 0, :, pl.ds(off, MOBA_BLOCK)].astype(BF16)
        acc = alpha * acc + jnp.dot(vj, p.astype(BF16), preferred_element_type=F32)
        return m_new, l, acc

    init = (jnp.full((1, QCOLS), NEG_INF, F32), jnp.zeros((1, QCOLS), F32),
            jnp.zeros((HEAD_DIM, QCOLS), F32))
    m, l, acc = lax.fori_loop(0, qb + 1, body, init)
    _unstack_heads(o_ref, acc / l)


def moba_attn_prompt(qt, k4, vt, tb):
    b, _, _, s = qt.shape
    n_blk = s // MOBA_BLOCK
    nt = s // MOBA_Q_BLOCK
    n_tb = tb.shape[1]
    return pl.pallas_call(
        functools.partial(_moba_prompt_kernel, n_blk=n_blk),
        grid=(N_KV_HEADS, b, nt),
        in_specs=[pl.BlockSpec((1, GROUP, HEAD_DIM, MOBA_Q_BLOCK), lambda g, b, t: (b, g, 0, t)),
                  pl.BlockSpec((1, 1, s, HEAD_DIM), lambda g, b, t: (b, g, 0, 0)),
                  pl.BlockSpec((1, 1, HEAD_DIM, s), lambda g, b, t: (b, g, 0, 0)),
                  pl.BlockSpec((1, n_tb, MOBA_BLOCK, QCOLS), lambda g, b, t: (g, 0, 0, 0))],
        out_specs=pl.BlockSpec((1, GROUP, HEAD_DIM, MOBA_Q_BLOCK), lambda g, b, t: (b, g, 0, t)),
        out_shape=jax.ShapeDtypeStruct(qt.shape, BF16),
        scratch_shapes=[pltpu.VMEM((n_blk, HEAD_DIM), F32), pltpu.VMEM((n_blk, QCOLS), F32)],
        compiler_params=_params(),
        name="moba_attn_prompt",
    )(qt, k4, vt, tb)


def _expand_heads(q, mask):
    return jnp.concatenate([q] * N_KV_HEADS, axis=1) * mask


def _fold_heads(o):
    acc = o
    for c in range(1, N_KV_HEADS):
        acc = acc + pltpu.roll(o, c * HEAD_DIM, axis=1)
    return acc[:, :HEAD_DIM]


def _win_sample_kernel(q_ref, kn_ref, vn_ref, kb_ref, vb_ref, bias_ref, bias0_ref, sink_ref,
                       mask_ref, o_ref, *, bb):
    mask = mask_ref[...]
    sink = sink_ref[...]
    for i in range(bb):
        qe = _expand_heads(q_ref[i] * ATTN_SCALE, mask)
        s = lax.dot_general(qe.astype(BF16), kb_ref[i].astype(BF16), (((1,), (1,)), ((), ())),
                            preferred_element_type=F32) + bias_ref[...]
        s0 = jnp.sum(qe * kn_ref[i], axis=1, keepdims=True) + bias0_ref[...]
        m = jnp.maximum(jnp.maximum(jnp.max(s, axis=1, keepdims=True), s0), sink)
        e = jnp.exp(s - m)
        e0 = jnp.exp(s0 - m)
        den = jnp.sum(e, axis=1, keepdims=True) + e0 + jnp.exp(sink - m)
        o = jnp.dot(e.astype(BF16), vb_ref[i].astype(BF16), preferred_element_type=F32) + e0 * vn_ref[i]
        o_ref[i] = _fold_heads(o * mask) / den


def window_attn_sample(q3, k_new, v_new, k_buf, v_buf, bias, bias0, sink, mask):
    db, w, kvw = k_buf.shape
    bb = _tile(db, 8)
    row = lambda i: (i, 0, 0)
    fixed = lambda i: (0, 0)
    return pl.pallas_call(
        functools.partial(_win_sample_kernel, bb=bb),
        grid=(db // bb,),
        in_specs=[pl.BlockSpec((bb, N_HEADS, HEAD_DIM), row),
                  pl.BlockSpec((bb, 1, kvw), row),
                  pl.BlockSpec((bb, 1, kvw), row),
                  pl.BlockSpec((bb, w, kvw), row),
                  pl.BlockSpec((bb, w, kvw), row),
                  pl.BlockSpec((N_HEADS, w), fixed),
                  pl.BlockSpec((N_HEADS, 1), fixed),
                  pl.BlockSpec((N_HEADS, 1), fixed),
                  pl.BlockSpec((N_HEADS, kvw), fixed)],
        out_specs=pl.BlockSpec((bb, N_HEADS, HEAD_DIM), row),
        out_shape=jax.ShapeDtypeStruct((db, N_HEADS, HEAD_DIM), F32),
        compiler_params=_params(),
        name="window_attn_sample",
    )(q3, k_new, v_new, k_buf, v_buf, bias, bias0, sink, mask)


def _copy_page(cache_ref, page, dst_ref, sem):
    return pltpu.make_async_copy(cache_ref.at[page], dst_ref, sem)


def _page_mean_kernel(pt_ref, cache_ref, o_ref, buf_ref, sem_ref, *, cpp, chunks_per_b):
    step = pl.program_id(0)
    n_steps = pl.num_programs(0)

    def copies(s, slot):
        b = s // chunks_per_b
        c = s % chunks_per_b
        return [_copy_page(cache_ref, pt_ref[b, c * cpp + p], buf_ref.at[slot, p], sem_ref.at[slot])
                for p in range(cpp)]

    @pl.when(step == 0)
    def _():
        for cp in copies(step, 0):
            cp.start()

    @pl.when(step + 1 < n_steps)
    def _():
        for cp in copies(step + 1, (step + 1) % 2):
            cp.start()

    slot = step % 2
    for cp in copies(step, slot):
        cp.wait()
    ppb = MOBA_BLOCK // PAGE_SIZE
    x = buf_ref[slot].reshape(cpp // ppb, MOBA_BLOCK, buf_ref.shape[-1])
    o_ref[0] = jnp.sum(x, axis=1) * (1.0 / MOBA_BLOCK)


def page_block_means(page_table, cache3, n_full):
    db = page_table.shape[0]
    kvw = cache3.shape[-1]
    ppb = MOBA_BLOCK // PAGE_SIZE
    n_pages = n_full * ppb
    cpp = _tile(n_pages, 64)
    chunks_per_b = n_pages // cpp
    return pl.pallas_call(
        functools.partial(_page_mean_kernel, cpp=cpp, chunks_per_b=chunks_per_b),
        grid_spec=pltpu.PrefetchScalarGridSpec(
            num_scalar_prefetch=1,
            grid=(db * chunks_per_b,),
            in_specs=[pl.BlockSpec(memory_space=pl.ANY)],
            out_specs=pl.BlockSpec((1, cpp // ppb, kvw),
                                   lambda s, pt: (s // chunks_per_b, s % chunks_per_b, 0)),
            scratch_shapes=[pltpu.VMEM((2, cpp, PAGE_SIZE, kvw), F32),
                            pltpu.SemaphoreType.DMA((2,))]),
        out_shape=jax.ShapeDtypeStruct((db, n_full, kvw), F32),
        compiler_params=_params(dimension_semantics=("arbitrary",)),
        name="page_block_means",
    )(page_table, cache3)


def _moba_select_kernel(q_ref, bm_ref, mask_ref, sel_ref, *, bb, n_sel):
    mask = mask_ref[...]
    for i in range(bb):
        qe = _expand_heads(q_ref[i], mask)
        gate = lax.dot_general(qe, bm_ref[i], (((1,), (1,)), ((), ())),
                               precision=lax.Precision.HIGHEST, preferred_element_type=F32)
        col = lax.broadcasted_iota(jnp.int32, gate.shape, 1)
        picks = []
        for _ in range(n_sel):
            best = jnp.max(gate, axis=1, keepdims=True)
            idx = jnp.min(jnp.where(gate == best, col, gate.shape[1]), axis=1, keepdims=True)
            picks.append(idx)
            gate = jnp.where(col == idx, -jnp.inf, gate)
        sel_ref[i] = jnp.concatenate(picks, axis=1)


def moba_select_sample(q3, blk_mean, mask, n_sel):
    db, n_full, kvw = blk_mean.shape
    bb = _tile(db, 8)
    return pl.pallas_call(
        functools.partial(_moba_select_kernel, bb=bb, n_sel=n_sel),
        grid=(db // bb,),
        in_specs=[pl.BlockSpec((bb, N_HEADS, HEAD_DIM), lambda i: (i, 0, 0)),
                  pl.BlockSpec((bb, n_full, kvw), lambda i: (i, 0, 0)),
                  pl.BlockSpec((N_HEADS, kvw), lambda i: (0, 0))],
        out_specs=pl.BlockSpec((bb, N_HEADS, n_sel), lambda i: (i, 0, 0)),
        out_shape=jax.ShapeDtypeStruct((db, N_HEADS, n_sel), jnp.int32),
        compiler_params=_params(),
        name="moba_select_sample",
    )(q3, blk_mean, mask)


def _moba_sample_kernel(phys_ref, tile_ref, ck_ref, cv_ref, q_ref, kn_ref, vn_ref, bt_ref, bias0_ref,
                        o_ref, kbuf_ref, vbuf_ref, sem_ref, *, n_sel):
    b = pl.program_id(0)
    n_b = pl.num_programs(0)
    ppb = MOBA_BLOCK // PAGE_SIZE
    n_pg = n_sel * ppb

    def copies(bi, slot):
        out = []
        for h in range(N_HEADS):
            lanes = pl.ds((h // GROUP) // KV_PER_CHUNK * LANES, LANES)
            for p in range(n_pg):
                page = phys_ref[bi, h * n_pg + p]
                rows = pl.ds(p * PAGE_SIZE, PAGE_SIZE)
                out.append(pltpu.make_async_copy(ck_ref.at[page, :, lanes], kbuf_ref.at[slot, h, rows, :],
                                                 sem_ref.at[0, slot]))
                out.append(pltpu.make_async_copy(cv_ref.at[page, :, lanes], vbuf_ref.at[slot, h, rows, :],
                                                 sem_ref.at[1, slot]))
        return out

    @pl.when(b == 0)
    def _():
        for cp in copies(b, 0):
            cp.start()

    @pl.when(b + 1 < n_b)
    def _():
        for cp in copies(b + 1, (b + 1) % 2):
            cp.start()

    slot = b % 2
    for cp in copies(b, slot):
        cp.wait()

    for h in range(N_HEADS):
        c = (h // GROUP) // KV_PER_CHUNK
        q = q_ref[0, h:h + 1, :] * ATTN_SCALE
        k = kbuf_ref[slot, h]
        s = jnp.sum(k * q, axis=1, keepdims=True)
        bias = jnp.concatenate([bt_ref[tile_ref[b, h * n_sel + si]][:, h:h + 1] for si in range(n_sel)],
                               axis=0)
        s = s + bias
        s0 = jnp.sum(q * kn_ref[0, c:c + 1, :], axis=1, keepdims=True) + bias0_ref[:, h:h + 1]
        m = jnp.maximum(jnp.max(s, axis=0, keepdims=True), s0)
        e = jnp.exp(s - m)
        e0 = jnp.exp(s0 - m)
        den = jnp.sum(e, axis=0, keepdims=True) + e0
        o = jnp.sum(e * vbuf_ref[slot, h], axis=0, keepdims=True) + e0 * vn_ref[0, c:c + 1, :]
        o_ref[0, h:h + 1, :] = o / den


def moba_attn_sample(phys, tiles, cache_k3, cache_v3, q_chunk, k_new, v_new, bias_tiles, bias0, n_sel):
    db = q_chunk.shape[0]
    n_keys = n_sel * MOBA_BLOCK
    n_chunks = k_new.shape[1]
    row = lambda b, *_: (b, 0, 0)
    return pl.pallas_call(
        functools.partial(_moba_sample_kernel, n_sel=n_sel),
        grid_spec=pltpu.PrefetchScalarGridSpec(
            num_scalar_prefetch=2,
            grid=(db,),
            in_specs=[pl.BlockSpec(memory_space=pl.ANY),
                      pl.BlockSpec(memory_space=pl.ANY),
                      pl.BlockSpec((1, N_HEADS, LANES), row),
                      pl.BlockSpec((1, n_chunks, LANES), row),
                      pl.BlockSpec((1, n_chunks, LANES), row),
                      pl.BlockSpec(bias_tiles.shape, lambda b, *_: (0, 0, 0)),
                      pl.BlockSpec((1, N_HEADS), lambda b, *_: (0, 0))],
            out_specs=pl.BlockSpec((1, N_HEADS, LANES), row),
            scratch_shapes=[pltpu.VMEM((2, N_HEADS, n_keys, LANES), F32),
                            pltpu.VMEM((2, N_HEADS, n_keys, LANES), F32),
                            pltpu.SemaphoreType.DMA((2, 2))]),
        out_shape=jax.ShapeDtypeStruct((db, N_HEADS, LANES), F32),
        compiler_params=_params(dimension_semantics=("arbitrary",)),
        name="moba_attn_sample",
    )(phys, tiles, cache_k3, cache_v3, q_chunk, k_new, v_new, bias_tiles, bias0)


def _router_kernel(x_ref, g_ref, wr_ref, xn_ref, gate_ref, idx_ref):
    xn = _rms(x_ref[...], g_ref[...])
    xn_ref[...] = xn.astype(BF16)
    logits = jnp.dot(xn, wr_ref[...], precision=lax.Precision.HIGHEST, preferred_element_type=F32)
    col = lax.broadcasted_iota(jnp.int32, logits.shape, 1)
    n_e = logits.shape[1]
    m1 = jnp.max(logits, axis=1, keepdims=True)
    i1 = jnp.min(jnp.where(logits == m1, col, n_e), axis=1, keepdims=True)
    rest = jnp.where(col == i1, -jnp.inf, logits)
    m2 = jnp.max(rest, axis=1, keepdims=True)
    i2 = jnp.min(jnp.where(rest == m2, col, n_e), axis=1, keepdims=True)
    e2 = jnp.exp(m2 - m1)
    den = 1.0 + e2
    gate_ref[...] = jnp.concatenate([1.0 / den, e2 / den], axis=1)
    idx_ref[...] = jnp.concatenate([i1, i2], axis=1)


def moe_router(x, g, w_router, tm):
    m, d = x.shape
    n_e = w_router.shape[1]
    return pl.pallas_call(
        _router_kernel,
        grid=(m // tm,),
        in_specs=[pl.BlockSpec((tm, d), lambda i: (i, 0)),
                  pl.BlockSpec((1, d), lambda i: (0, 0)),
                  pl.BlockSpec((d, n_e), lambda i: (0, 0))],
        out_specs=[pl.BlockSpec((tm, d), lambda i: (i, 0)),
                   pl.BlockSpec((tm, EXPERT_TOPK), lambda i: (i, 0)),
                   pl.BlockSpec((tm, EXPERT_TOPK), lambda i: (i, 0))],
        out_shape=[jax.ShapeDtypeStruct((m, d), BF16),
                   jax.ShapeDtypeStruct((m, EXPERT_TOPK), F32),
                   jax.ShapeDtypeStruct((m, EXPERT_TOPK), jnp.int32)],
        compiler_params=_params(),
        name="moe_router",
    )(x, g.reshape(1, d), w_router)


def _moe_ffn_kernel(te_ref, live_ref, x_ref, wg_ref, wu_ref, wd_ref, o_ref, acc_ref):
    i = pl.program_id(0)
    j = pl.program_id(1)

    @pl.when(j == 0)
    def _():
        acc_ref[...] = jnp.zeros_like(acc_ref)

    @pl.when(live_ref[i] > 0)
    def _():
        x = x_ref[...]
        a = jnp.dot(x, wg_ref[0], preferred_element_type=F32)
        b = jnp.dot(x, wu_ref[0], preferred_element_type=F32)
        h = (a * jax.nn.sigmoid(a)) * b
        acc_ref[...] += jnp.dot(h.astype(BF16), wd_ref[0], preferred_element_type=F32)

    @pl.when(j == pl.num_programs(1) - 1)
    def _():
        o_ref[...] = acc_ref[...]


def moe_ffn(tile_expert, tile_live, x_sorted, wg, wu, wd, tm, tn):
    p, d = x_sorted.shape
    f = wg.shape[2]
    return pl.pallas_call(
        _moe_ffn_kernel,
        grid_spec=pltpu.PrefetchScalarGridSpec(
            num_scalar_prefetch=2,
            grid=(p // tm, f // tn),
            in_specs=[pl.BlockSpec((tm, d), lambda i, j, te, lv: (i, 0)),
                      pl.BlockSpec((1, d, tn), lambda i, j, te, lv: (te[i], 0, j)),
                      pl.BlockSpec((1, d, tn), lambda i, j, te, lv: (te[i], 0, j)),
                      pl.BlockSpec((1, tn, d), lambda i, j, te, lv: (te[i], j, 0))],
            out_specs=pl.BlockSpec((tm, d), lambda i, j, te, lv: (i, 0)),
            scratch_shapes=[pltpu.VMEM((tm, d), F32)]),
        out_shape=jax.ShapeDtypeStruct((p, d), F32),
        compiler_params=_params(),
        name="moe_ffn",
    )(tile_expert, tile_live, x_sorted, wg, wu, wd)


def _combine_norm_kernel(y_ref, e1_ref, e2_ref, gate_ref, g_ref, o_ref):
    gate = gate_ref[...]
    y = y_ref[...] + (gate[:, 0:1] * e1_ref[...] + gate[:, 1:2] * e2_ref[...])
    o_ref[...] = _rms(y, g_ref[...])


def combine_norm(y, e1, e2, gates, g, tm):
    m, d = y.shape
    blk = pl.BlockSpec((tm, d), lambda i: (i, 0))
    return pl.pallas_call(
        _combine_norm_kernel,
        grid=(m // tm,),
        in_specs=[blk, blk, blk,
                  pl.BlockSpec((tm, EXPERT_TOPK), lambda i: (i, 0)),
                  pl.BlockSpec((1, d), lambda i: (0, 0))],
        out_specs=blk,
        out_shape=jax.ShapeDtypeStruct((m, d), F32),
        compiler_params=_params(),
        name="combine_norm",
    )(y, e1, e2, gates, g.reshape(1, d))


def moe_dispatch_plan(idx, tm):
    m, k = idx.shape
    n_asg = m * k
    p = -(-(n_asg + N_EXPERTS * (tm - 1)) // tm) * tm
    flat = idx.reshape(n_asg)
    onehot = (flat[:, None] == jnp.arange(N_EXPERTS, dtype=jnp.int32)[None, :]).astype(jnp.int32)
    csum = jnp.cumsum(onehot, axis=0)
    count = csum[-1]
    rank = jnp.take_along_axis(csum, flat[:, None], axis=1)[:, 0] - 1
    padded = -(-count // tm) * tm
    pad_end = jnp.cumsum(padded)
    pad_start = pad_end - padded
    slot = pad_start[flat] + rank
    order = jnp.sort(flat * n_asg + jnp.arange(n_asg, dtype=jnp.int32)) % n_asg
    start = jnp.cumsum(count) - count
    slots = jnp.arange(p, dtype=jnp.int32)
    slot_e = jnp.minimum(jnp.searchsorted(pad_end, slots, side="right"), N_EXPERTS - 1).astype(jnp.int32)
    r = slots - pad_start[slot_e]
    live = r < count[slot_e]
    src_asg = order[jnp.clip(start[slot_e] + r, 0, n_asg - 1)]
    row_src = jnp.where(live, src_asg // k, 0).astype(jnp.int32)
    tile_expert = slot_e[::tm]
    tile_live = live[::tm].astype(jnp.int32)
    return row_src, slot.reshape(m, k), tile_expert, tile_live


def _window_prompt_bias(table):
    kk = np.arange(2 * BAND_BLOCK)[:, None]
    i = np.arange(BAND_BLOCK)[None, :]
    dist = BAND_BLOCK + i - kk
    valid = (dist >= 0) & (dist <= WINDOW)
    bias = table[_bucket_np(dist)]
    bias = jnp.where(valid[..., None], bias, NEG_INF)
    bias = bias.reshape(2 * BAND_BLOCK, BAND_BLOCK, N_KV_HEADS, GROUP).transpose(2, 0, 3, 1)
    return bias.reshape(N_KV_HEADS, 2 * BAND_BLOCK, QCOLS)


def _moba_prompt_bias(table):
    u = np.arange(MOBA_FAR_TILE + 1)[:, None, None]
    kk = np.arange(MOBA_BLOCK)[None, :, None]
    i = np.arange(MOBA_Q_BLOCK)[None, None, :]
    dist = u * MOBA_Q_BLOCK + i - kk
    dist = np.where(u == MOBA_FAR_TILE, np.maximum(dist, FAR_DIST), dist)
    bias = table[_bucket_np(dist)]
    bias = jnp.where((dist >= 0)[..., None], bias, NEG_INF)
    n_t = MOBA_FAR_TILE + 1
    bias = bias.reshape(n_t, MOBA_BLOCK, MOBA_Q_BLOCK, N_KV_HEADS, GROUP).transpose(3, 0, 1, 4, 2)
    return bias.reshape(N_KV_HEADS, n_t, MOBA_BLOCK, QCOLS)


def _moba_sample_bias(table, past_len):
    ti = np.arange(SAMPLE_FAR_TILE + 1)[:, None]
    r = np.arange(MOBA_BLOCK)[None, :]
    dist = (ti + 1) * MOBA_BLOCK - r + (past_len % MOBA_BLOCK)
    dist = np.where(ti == SAMPLE_FAR_TILE, np.maximum(dist, FAR_DIST), dist)
    return table[_bucket_np(dist)]


def kernel(x_prompt, x_sample, state_win_k, state_win_v, cache_k, cache_v, page_table, rel_bias_table,
           attn_norm_g, ffn_norm_g, w_qkv_a, sink_a, w_o_a, kv_norm_g, w_kv_shared, w_q_b, w_o_b,
           w_gate_dense, w_up_dense, w_down_dense, w_router, w_gate_moe, w_up_moe, w_down_moe, final_norm_g):
    bp, sp, d = x_prompt.shape
    bs, ss, _ = x_sample.shape
    assert ss == 1 and attn_norm_g.shape[0] == 2
    mp = bp * sp
    hq = N_HEADS * HEAD_DIM
    hk = N_KV_HEADS * HEAD_DIM
    past_len = page_table.shape[1] * PAGE_SIZE
    assert past_len % MOBA_BLOCK == 0 and sp % MOBA_BLOCK == 0
    n_full = past_len // MOBA_BLOCK
    n_sel = min(MOBA_TOPK, n_full)
    w_buf = state_win_k.shape[2]
    table = rel_bias_table.astype(F32)

    tm_p = _tile(mp, 512)
    tm_s = _tile(bs, 512)
    tn = _tile(w_gate_dense.shape[2], 896) if w_gate_dense.shape[2] % 128 == 0 else w_gate_dense.shape[2]
    cast = lambda w: w.astype(BF16)

    yp = x_prompt.reshape(mp, d)
    ys = x_sample.reshape(bs, d)

    w_qkv = cast(w_qkv_a[0])
    qkv_p = rms_matmul(yp, attn_norm_g[0], w_qkv, tm_p)
    qkv_s = rms_matmul(ys, attn_norm_g[0], w_qkv, tm_s)

    def prompt_layouts(q, k, v):
        qt = q.reshape(bp, sp, N_HEADS, HEAD_DIM).transpose(0, 2, 3, 1)
        k4 = k.reshape(bp, sp, N_KV_HEADS, HEAD_DIM).transpose(0, 2, 1, 3)
        vt = v.reshape(bp, sp, N_KV_HEADS, HEAD_DIM).transpose(0, 2, 3, 1)
        return qt, k4, vt

    def from_heads_t(ot):
        return ot.transpose(0, 3, 1, 2).reshape(mp, hq)

    kp = qkv_p[:, hq:hq + hk]
    vp = qkv_p[:, hq + hk:]
    qt, k4, vt = prompt_layouts(qkv_p[:, :hq], kp, vp)
    sink_cols = jnp.repeat(sink_a[0].astype(F32).reshape(N_KV_HEADS, 1, GROUP), BAND_BLOCK, axis=2)
    mix_p = from_heads_t(window_attn_prompt(qt, k4, vt, _window_prompt_bias(table), sink_cols))

    head_mask = jnp.asarray((np.arange(N_HEADS)[:, None] // GROUP == np.arange(hk)[None, :] // HEAD_DIM)
                            .astype(np.float32))
    ks_new = qkv_s[:, hq:hq + hk]
    vs_new = qkv_s[:, hq + hk:]
    k_buf = state_win_k[:, 0].reshape(bs, w_buf, hk)
    v_buf = state_win_v[:, 0].reshape(bs, w_buf, hk)
    bias_ws = table[_bucket_np(w_buf - np.arange(w_buf))].T
    bias_0 = table[0][:, None]
    mix_s = window_attn_sample(qkv_s[:, :hq].reshape(bs, N_HEADS, HEAD_DIM), ks_new[:, None], vs_new[:, None],
                               k_buf, v_buf, bias_ws, bias_0, sink_a[0].astype(F32)[:, None], head_mask)

    w_o = cast(w_o_a[0])
    yp = matmul_res(mix_p, w_o, yp, tm_p)
    ys = matmul_res(mix_s.reshape(bs, hq), w_o, ys, tm_s)

    wg, wu, wd = cast(w_gate_dense[0]), cast(w_up_dense[0]), cast(w_down_dense[0])
    yp = ffn_dense(yp, ffn_norm_g[0], wg, wu, wd, tm_p, tn)
    ys = ffn_dense(ys, ffn_norm_g[0], wg, wu, wd, tm_s, tn)

    w_kv = cast(w_kv_shared)
    kv_p = rms_matmul(yp, kv_norm_g, w_kv, tm_p)
    kv_s = rms_matmul(ys, kv_norm_g, w_kv, tm_s)
    w_q = cast(w_q_b[0])
    q_p = rms_matmul(yp, attn_norm_g[1], w_q, tm_p)
    q_s = rms_matmul(ys, attn_norm_g[1], w_q, tm_s)

    qt, k4, vt = prompt_layouts(q_p, kv_p[:, :hk], kv_p[:, hk:])
    mix_p = from_heads_t(moba_attn_prompt(qt, k4, vt, _moba_prompt_bias(table)))

    q3 = q_s.reshape(bs, N_HEADS, HEAD_DIM)
    n_phys = cache_k.shape[0]
    cache_k3 = cache_k.reshape(n_phys, PAGE_SIZE, hk)
    cache_v3 = cache_v.reshape(n_phys, PAGE_SIZE, hk)
    blk_mean = page_block_means(page_table, cache_k3, n_full)
    sel = moba_select_sample(q3, blk_mean, head_mask, n_sel)
    ppb = MOBA_BLOCK // PAGE_SIZE
    pages = sel[..., None] * ppb + jnp.arange(ppb, dtype=jnp.int32)
    phys = jnp.take_along_axis(page_table, pages.reshape(bs, -1), axis=1)
    tiles = jnp.clip(n_full - 1 - sel, 0, SAMPLE_FAR_TILE).reshape(bs, -1)
    half = (np.arange(N_HEADS) // GROUP) % KV_PER_CHUNK
    half_mask = jnp.asarray((half[:, None] == np.arange(LANES)[None, :] // HEAD_DIM).astype(np.float32))
    q_chunk = jnp.concatenate([q3] * KV_PER_CHUNK, axis=2) * half_mask
    o_chunk = moba_attn_sample(phys, tiles, cache_k3, cache_v3, q_chunk,
                               kv_s[:, :hk].reshape(bs, hk // LANES, LANES),
                               kv_s[:, hk:].reshape(bs, hk // LANES, LANES),
                               _moba_sample_bias(table, past_len), table[0][None, :], n_sel)
    o_chunk = o_chunk.reshape(bs, N_HEADS, KV_PER_CHUNK, HEAD_DIM)
    mix_s = jnp.take_along_axis(o_chunk, jnp.asarray(half)[None, :, None, None], axis=2)[:, :, 0]

    w_o = cast(w_o_b[0])
    yp = matmul_res(mix_p, w_o, yp, tm_p)
    ys = matmul_res(mix_s.reshape(bs, hq), w_o, ys, tm_s)

    y_all = jnp.concatenate([yp, ys], axis=0)
    m_all = mp + bs
    tm_a = _tile(m_all, 512)
    xn, gates, idx = moe_router(y_all, ffn_norm_g[1], w_router[0].astype(F32), tm_a)
    tm_e = 512
    row_src, slot, tile_expert, tile_live = moe_dispatch_plan(idx, tm_e)
    x_sorted = jnp.take(xn, row_src, axis=0)
    e_sorted = moe_ffn(tile_expert, tile_live, x_sorted, cast(w_gate_moe[0]), cast(w_up_moe[0]),
                       cast(w_down_moe[0]), tm_e, tn)
    e1 = jnp.take(e_sorted, slot[:, 0], axis=0)
    e2 = jnp.take(e_sorted, slot[:, 1], axis=0)
    y_out = combine_norm(y_all, e1, e2, gates, final_norm_g, tm_a)

    y_prompt = y_out[:mp].reshape(bp, sp, d)
    y_sample = y_out[mp:].reshape(bs, ss, d)
    w_keep = min(WINDOW, sp)
    kv4 = lambda t, b, s: t.reshape(b, s, N_KV_HEADS, HEAD_DIM)
    win_k_prompt = kv4(kp, bp, sp)[:, None, sp - w_keep:]
    win_v_prompt = kv4(vp, bp, sp)[:, None, sp - w_keep:]
    win_k_sample = jnp.concatenate([state_win_k[:, 0], kv4(ks_new, bs, 1)], axis=1)[:, None, ss:]
    win_v_sample = jnp.concatenate([state_win_v[:, 0], kv4(vs_new, bs, 1)], axis=1)[:, None, ss:]
    return (y_prompt, y_sample, win_k_prompt, win_v_prompt,
            kv4(kv_p[:, :hk], bp, sp), kv4(kv_p[:, hk:], bp, sp),
            win_k_sample, win_v_sample,
            kv4(kv_s[:, :hk], bs, 1), kv4(kv_s[:, hk:], bs, 1))
```

```python
import functools
import math

import numpy as np
import jax
import jax.numpy as jnp
from jax import lax
from jax.experimental import pallas as pl
from jax.experimental.pallas import tpu as pltpu

N_HEADS = 16
N_KV_HEADS = 4
HEAD_DIM = 64
GROUP = N_HEADS // N_KV_HEADS
WINDOW = 128
BAND_BLOCK = 128
MOBA_BLOCK = 256
MOBA_TOPK = 3
MOBA_Q_BLOCK = 128
PAGE_SIZE = 128
N_BUCKETS = 32
REL_MAX_DIST = 1024
N_EXPERTS = 8
EXPERT_TOPK = 2
RMS_EPS = 1e-5
NEG_INF = -1e30
ATTN_SCALE = HEAD_DIM ** -0.5
QCOLS = GROUP * MOBA_Q_BLOCK
FAR_DIST = 790
MOBA_FAR_TILE = 9
SAMPLE_FAR_TILE = 4

F32 = jnp.float32
BF16 = jnp.bfloat16
VMEM_LIMIT = 56 * 1024 * 1024


def _bucket_np(dist):
    max_exact = N_BUCKETS // 2
    n = np.maximum(np.asarray(dist, np.int64), 0)
    nf = np.maximum(n, 1).astype(np.float64)
    large = max_exact + (np.log(nf / max_exact) / math.log(REL_MAX_DIST / max_exact)
                         * (N_BUCKETS - max_exact)).astype(np.int64)
    return np.where(n < max_exact, n, np.minimum(large, N_BUCKETS - 1)).astype(np.int32)


def _tile(n, pref):
    if n <= pref:
        return n
    for t in range(pref, 7, -8):
        if n % t == 0:
            return t
    return n


def _params(**kw):
    return pltpu.CompilerParams(vmem_limit_bytes=VMEM_LIMIT, **kw)


def _rms(x, g):
    return (x * lax.rsqrt(jnp.mean(x * x, axis=-1, keepdims=True) + RMS_EPS)) * g


def _rms_matmul_kernel(x_ref, g_ref, w_ref, o_ref):
    xn = _rms(x_ref[...], g_ref[...]).astype(BF16)
    o_ref[...] = jnp.dot(xn, w_ref[...], preferred_element_type=F32)


def rms_matmul(x, g, w, tm):
    m, d = x.shape
    n = w.shape[1]
    return pl.pallas_call(
        _rms_matmul_kernel,
        grid=(m // tm,),
        in_specs=[pl.BlockSpec((tm, d), lambda i: (i, 0)),
                  pl.BlockSpec((1, d), lambda i: (0, 0)),
                  pl.BlockSpec((d, n), lambda i: (0, 0))],
        out_specs=pl.BlockSpec((tm, n), lambda i: (i, 0)),
        out_shape=jax.ShapeDtypeStruct((m, n), F32),
        compiler_params=_params(),
        name="rms_matmul",
    )(x, g.reshape(1, d), w)


def _matmul_res_kernel(a_ref, w_ref, r_ref, o_ref):
    o_ref[...] = r_ref[...] + jnp.dot(a_ref[...].astype(BF16), w_ref[...], preferred_element_type=F32)


def matmul_res(a, w, res, tm):
    m, k = a.shape
    n = w.shape[1]
    return pl.pallas_call(
        _matmul_res_kernel,
        grid=(m // tm,),
        in_specs=[pl.BlockSpec((tm, k), lambda i: (i, 0)),
                  pl.BlockSpec((k, n), lambda i: (0, 0)),
                  pl.BlockSpec((tm, n), lambda i: (i, 0))],
        out_specs=pl.BlockSpec((tm, n), lambda i: (i, 0)),
        out_shape=jax.ShapeDtypeStruct((m, n), F32),
        compiler_params=_params(),
        name="matmul_res",
    )(a, w, res)


def _ffn_kernel(x_ref, g_ref, wg_ref, wu_ref, wd_ref, o_ref, xn_ref, acc_ref):
    j = pl.program_id(1)

    @pl.when(j == 0)
    def _():
        xn_ref[...] = _rms(x_ref[...], g_ref[...]).astype(BF16)
        acc_ref[...] = jnp.zeros_like(acc_ref)

    xn = xn_ref[...]
    a = jnp.dot(xn, wg_ref[...], preferred_element_type=F32)
    b = jnp.dot(xn, wu_ref[...], preferred_element_type=F32)
    h = (a * jax.nn.sigmoid(a)) * b
    acc_ref[...] += jnp.dot(h.astype(BF16), wd_ref[...], preferred_element_type=F32)

    @pl.when(j == pl.num_programs(1) - 1)
    def _():
        o_ref[...] = x_ref[...] + acc_ref[...]


def ffn_dense(x, g, wg, wu, wd, tm, tn):
    m, d = x.shape
    f = wg.shape[1]
    return pl.pallas_call(
        _ffn_kernel,
        grid=(m // tm, f // tn),
        in_specs=[pl.BlockSpec((tm, d), lambda i, j: (i, 0)),
                  pl.BlockSpec((1, d), lambda i, j: (0, 0)),
                  pl.BlockSpec((d, tn), lambda i, j: (0, j)),
                  pl.BlockSpec((d, tn), lambda i, j: (0, j)),
                  pl.BlockSpec((tn, d), lambda i, j: (j, 0))],
        out_specs=pl.BlockSpec((tm, d), lambda i, j: (i, 0)),
        out_shape=jax.ShapeDtypeStruct((m, d), F32),
        scratch_shapes=[pltpu.VMEM((tm, d), BF16), pltpu.VMEM((tm, d), F32)],
        compiler_params=_params(),
        name="ffn_dense",
    )(x, g.reshape(1, d), wg, wu, wd)


def _stack_heads(q_ref, g=0):
    return jnp.concatenate([q_ref[0, g * GROUP + r] for r in range(GROUP)], axis=1)


def _unstack_heads(o_ref, o, g=0):
    for r in range(GROUP):
        o_ref[0, g * GROUP + r] = o[:, r * MOBA_Q_BLOCK:(r + 1) * MOBA_Q_BLOCK].astype(o_ref.dtype)


def _win_prompt_kernel(q_ref, kp_ref, ko_ref, vp_ref, vo_ref, wb_ref, sink_ref, o_ref):
    n = pl.program_id(1)
    row = lax.broadcasted_iota(jnp.int32, (2 * BAND_BLOCK, QCOLS), 0)
    has_prev = (row >= BAND_BLOCK) | (n > 0)
    for g in range(N_KV_HEADS):
        q = (_stack_heads(q_ref, g) * ATTN_SCALE).astype(BF16)
        k = jnp.concatenate([kp_ref[0, g], ko_ref[0, g]], axis=0).astype(BF16)
        s = jnp.dot(k, q, preferred_element_type=F32) + wb_ref[g]
        s = jnp.where(has_prev, s, NEG_INF)
        sink = sink_ref[g]
        m = jnp.maximum(jnp.max(s, axis=0, keepdims=True), sink)
        e = jnp.exp(s - m)
        den = jnp.sum(e, axis=0, keepdims=True) + jnp.exp(sink - m)
        vt = jnp.concatenate([vp_ref[0, g], vo_ref[0, g]], axis=1).astype(BF16)
        o = jnp.dot(vt, e.astype(BF16), preferred_element_type=F32) / den
        _unstack_heads(o_ref, o, g)


def window_attn_prompt(qt, k4, vt, wb, sink_cols):
    b, _, _, s = qt.shape
    nb = s // BAND_BLOCK
    qb = BAND_BLOCK
    prev = lambda n: jnp.maximum(n - 1, 0)
    return pl.pallas_call(
        _win_prompt_kernel,
        grid=(b, nb),
        in_specs=[pl.BlockSpec((1, N_HEADS, HEAD_DIM, qb), lambda b, n: (b, 0, 0, n)),
                  pl.BlockSpec((1, N_KV_HEADS, qb, HEAD_DIM), lambda b, n: (b, 0, prev(n), 0)),
                  pl.BlockSpec((1, N_KV_HEADS, qb, HEAD_DIM), lambda b, n: (b, 0, n, 0)),
                  pl.BlockSpec((1, N_KV_HEADS, HEAD_DIM, qb), lambda b, n: (b, 0, 0, prev(n))),
                  pl.BlockSpec((1, N_KV_HEADS, HEAD_DIM, qb), lambda b, n: (b, 0, 0, n)),
                  pl.BlockSpec((N_KV_HEADS, 2 * qb, QCOLS), lambda b, n: (0, 0, 0)),
                  pl.BlockSpec((N_KV_HEADS, 1, QCOLS), lambda b, n: (0, 0, 0))],
        out_specs=pl.BlockSpec((1, N_HEADS, HEAD_DIM, qb), lambda b, n: (b, 0, 0, n)),
        out_shape=jax.ShapeDtypeStruct(qt.shape, BF16),
        compiler_params=_params(),
        name="window_attn_prompt",
    )(qt, k4, k4, vt, vt, wb, sink_cols)


def _moba_prompt_kernel(q_ref, k_ref, vt_ref, tb_ref, o_ref, kmean_ref, sel_ref, *, n_blk):
    t = pl.program_id(2)
    qb = t // (MOBA_BLOCK // MOBA_Q_BLOCK)

    @pl.when(t == 0)
    def _():
        kf = k_ref[0, 0].reshape(n_blk, MOBA_BLOCK, HEAD_DIM)
        kmean_ref[...] = jnp.mean(kf, axis=1)

    q32 = _stack_heads(q_ref)
    gate = jnp.dot(kmean_ref[...].astype(BF16), q32.astype(BF16),
                   preferred_element_type=F32)
    blk = lax.broadcasted_iota(jnp.int32, gate.shape, 0)
    past = blk < qb
    gate = jnp.where(past, gate, NEG_INF)
    rank = jnp.zeros(gate.shape, jnp.int32)
    for kk in range(n_blk):
        gk = gate[kk:kk + 1, :]
        beats = (gk > gate) | ((gk == gate) & (kk < blk))
        rank = rank + beats.astype(jnp.int32)
    sel = (past & (rank < MOBA_TOPK)) | (blk == qb)
    sel_ref[...] = sel.astype(F32)
    qs = (q32 * ATTN_SCALE).astype(BF16)

    tiles_per_blk = MOBA_BLOCK // MOBA_Q_BLOCK

    def masked_logits(s, j):
        u = jnp.clip(t - tiles_per_blk * j, 0, MOBA_FAR_TILE)
        return jnp.where(sel_ref[pl.ds(j, 1), :] > 0, s + tb_ref[0, u], NEG_INF)

    def body(jp, carry):
        m, l, acc = carry
        off = pl.multiple_of(jp * (2 * MOBA_BLOCK), 2 * MOBA_BLOCK)
        kj = k_ref[0, 0, pl.ds(off, 2 * MOBA_BLOCK), :].astype(BF16)
        s = jnp.dot(kj, qs, preferred_element_type=F32)
        s0 = masked_logits(s[:MOBA_BLOCK], 2 * jp)
        s1 = masked_logits(s[MOBA_BLOCK:], 2 * jp + 1)
        m_new = jnp.maximum(m, jnp.maximum(jnp.max(s0, axis=0, keepdims=True),
                                           jnp.max(s1, axis=0, keepdims=True)))
        alpha = jnp.exp(m - m_new)
        p0 = jnp.exp(s0 - m_new)
        p1 = jnp.exp(s1 - m_new)
        l = alpha * l + (jnp.sum(p0, axis=0, keepdims=True) + jnp.sum(p1, axis=0, keepdims=True))
        vj = vt_ref[0, 0, :, pl.ds(off, 2 * MOBA_BLOCK)].astype(BF16)
        p = jnp.concatenate([p0, p1], axis=0).astype(BF16)
        acc = alpha * acc + jnp.dot(vj, p, preferred_element_type=F32)
        return m_new, l, acc

    init = (jnp.full((1, QCOLS), NEG_INF, F32), jnp.zeros((1, QCOLS), F32),
            jnp.zeros((HEAD_DIM, QCOLS), F32))
    m, l, acc = lax.fori_loop(0, qb // 2 + 1, body, init)
    _unstack_heads(o_ref, acc / l)


def moba_attn_prompt(qt, k4, vt, tb):
    b, _, _, s = qt.shape
    n_blk = s // MOBA_BLOCK
    assert n_blk % 2 == 0, "kv blocks are swept in pairs"
    nt = s // MOBA_Q_BLOCK
    n_tb = tb.shape[1]
    return pl.pallas_call(
        functools.partial(_moba_prompt_kernel, n_blk=n_blk),
        grid=(N_KV_HEADS, b, nt),
        in_specs=[pl.BlockSpec((1, GROUP, HEAD_DIM, MOBA_Q_BLOCK), lambda g, b, t: (b, g, 0, t)),
                  pl.BlockSpec((1, 1, s, HEAD_DIM), lambda g, b, t: (b, g, 0, 0)),
                  pl.BlockSpec((1, 1, HEAD_DIM, s), lambda g, b, t: (b, g, 0, 0)),
                  pl.BlockSpec((1, n_tb, MOBA_BLOCK, QCOLS), lambda g, b, t: (g, 0, 0, 0))],
        out_specs=pl.BlockSpec((1, GROUP, HEAD_DIM, MOBA_Q_BLOCK), lambda g, b, t: (b, g, 0, t)),
        out_shape=jax.ShapeDtypeStruct(qt.shape, BF16),
        scratch_shapes=[pltpu.VMEM((n_blk, HEAD_DIM), F32), pltpu.VMEM((n_blk, QCOLS), F32)],
        compiler_params=_params(),
        name="moba_attn_prompt",
    )(qt, k4, vt, tb)


def _expand_heads(q, mask):
    return jnp.concatenate([q] * N_KV_HEADS, axis=1) * mask


def _fold_heads(o):
    acc = o
    for c in range(1, N_KV_HEADS):
        acc = acc + pltpu.roll(o, c * HEAD_DIM, axis=1)
    return acc[:, :HEAD_DIM]


def _win_sample_kernel(q_ref, kn_ref, vn_ref, kb_ref, vb_ref, bias_ref, bias0_ref, sink_ref,
                       mask_ref, o_ref, *, bb):
    mask = mask_ref[...]
    sink = sink_ref[...]
    for i in range(bb):
        qe = _expand_heads(q_ref[i] * ATTN_SCALE, mask)
        s = lax.dot_general(qe.astype(BF16), kb_ref[i].astype(BF16), (((1,), (1,)), ((), ())),
                            preferred_element_type=F32) + bias_ref[...]
        s0 = jnp.sum(qe * kn_ref[i], axis=1, keepdims=True) + bias0_ref[...]
        m = jnp.maximum(jnp.maximum(jnp.max(s, axis=1, keepdims=True), s0), sink)
        e = jnp.exp(s - m)
        e0 = jnp.exp(s0 - m)
        den = jnp.sum(e, axis=1, keepdims=True) + e0 + jnp.exp(sink - m)
        o = jnp.dot(e.astype(BF16), vb_ref[i].astype(BF16), preferred_element_type=F32) + e0 * vn_ref[i]
        o_ref[i] = _fold_heads(o * mask) / den


def window_attn_sample(q3, k_new, v_new, k_buf, v_buf, bias, bias0, sink, mask):
    db, w, kvw = k_buf.shape
    bb = _tile(db, 8)
    row = lambda i: (i, 0, 0)
    fixed = lambda i: (0, 0)
    return pl.pallas_call(
        functools.partial(_win_sample_kernel, bb=bb),
        grid=(db // bb,),
        in_specs=[pl.BlockSpec((bb, N_HEADS, HEAD_DIM), row),
                  pl.BlockSpec((bb, 1, kvw), row),
                  pl.BlockSpec((bb, 1, kvw), row),
                  pl.BlockSpec((bb, w, kvw), row),
                  pl.BlockSpec((bb, w, kvw), row),
                  pl.BlockSpec((N_HEADS, w), fixed),
                  pl.BlockSpec((N_HEADS, 1), fixed),
                  pl.BlockSpec((N_HEADS, 1), fixed),
                  pl.BlockSpec((N_HEADS, kvw), fixed)],
        out_specs=pl.BlockSpec((bb, N_HEADS, HEAD_DIM), row),
        out_shape=jax.ShapeDtypeStruct((db, N_HEADS, HEAD_DIM), F32),
        compiler_params=_params(),
        name="window_attn_sample",
    )(q3, k_new, v_new, k_buf, v_buf, bias, bias0, sink, mask)


def _copy_page(cache_ref, page, dst_ref, sem):
    return pltpu.make_async_copy(cache_ref.at[page], dst_ref, sem)


def _page_gate_kernel(pt_ref, cache_ref, qm_ref, o_ref, buf_ref, sem_ref, *, cpp, chunks_per_b):
    step = pl.program_id(0)
    n_steps = pl.num_programs(0)

    def copies(s, slot):
        b = s // chunks_per_b
        c = s % chunks_per_b
        return [_copy_page(cache_ref, pt_ref[b, c * cpp + p], buf_ref.at[slot, p], sem_ref.at[slot])
                for p in range(cpp)]

    @pl.when(step == 0)
    def _():
        for cp in copies(step, 0):
            cp.start()

    @pl.when(step + 1 < n_steps)
    def _():
        for cp in copies(step + 1, (step + 1) % 2):
            cp.start()

    slot = step % 2
    for cp in copies(step, slot):
        cp.wait()
    ppb = MOBA_BLOCK // PAGE_SIZE
    qm = qm_ref[0].astype(BF16).astype(F32)
    for n in range(cpp // ppb):
        x = jnp.sum(buf_ref[slot, pl.ds(n * ppb, ppb)], axis=0)
        mean = jnp.sum(x, axis=-1, keepdims=True) * (1.0 / MOBA_BLOCK)
        mean = mean.astype(BF16).astype(F32)
        o_ref[0, n:n + 1, :] = jnp.sum(jnp.sum(mean * qm, axis=0), axis=0, keepdims=True)


def page_block_gates(page_table, cache_t, qm, n_full):
    db = page_table.shape[0]
    ppb = MOBA_BLOCK // PAGE_SIZE
    n_pages = n_full * ppb
    cpp = _tile(n_pages, 64)
    chunks_per_b = n_pages // cpp
    return pl.pallas_call(
        functools.partial(_page_gate_kernel, cpp=cpp, chunks_per_b=chunks_per_b),
        grid_spec=pltpu.PrefetchScalarGridSpec(
            num_scalar_prefetch=1,
            grid=(db * chunks_per_b,),
            in_specs=[pl.BlockSpec(memory_space=pl.ANY),
                      pl.BlockSpec((1, N_KV_HEADS, HEAD_DIM, N_HEADS),
                                   lambda s, pt: (s // chunks_per_b, 0, 0, 0))],
            out_specs=pl.BlockSpec((1, cpp // ppb, N_HEADS),
                                   lambda s, pt: (s // chunks_per_b, s % chunks_per_b, 0)),
            scratch_shapes=[pltpu.VMEM((2, cpp, N_KV_HEADS, HEAD_DIM, PAGE_SIZE), F32),
                            pltpu.SemaphoreType.DMA((2,))]),
        out_shape=jax.ShapeDtypeStruct((db, n_full, N_HEADS), F32),
        compiler_params=_params(dimension_semantics=("arbitrary",)),
        name="page_block_gates",
    )(page_table, cache_t, qm)


def _moba_select_kernel(gate_ref, sel_ref, *, bb, n_sel):
    for i in range(bb):
        gate = gate_ref[i]
        n_full = gate.shape[0]
        row = lax.broadcasted_iota(jnp.int32, gate.shape, 0).astype(F32)
        picks = []
        for _ in range(n_sel):
            best = jnp.max(gate, axis=0, keepdims=True)
            idx = jnp.min(jnp.where(gate == best, row, float(n_full)), axis=0, keepdims=True)
            picks.append(idx)
            gate = jnp.where(row == idx, -jnp.inf, gate)
        sel_ref[i] = jnp.concatenate(picks, axis=0).astype(jnp.int32)


def moba_select_sample(gates, n_sel):
    db, n_full, _ = gates.shape
    bb = _tile(db, 8)
    return pl.pallas_call(
        functools.partial(_moba_select_kernel, bb=bb, n_sel=n_sel),
        grid=(db // bb,),
        in_specs=[pl.BlockSpec((bb, n_full, N_HEADS), lambda i: (i, 0, 0))],
        out_specs=pl.BlockSpec((bb, n_sel, N_HEADS), lambda i: (i, 0, 0)),
        out_shape=jax.ShapeDtypeStruct((db, n_sel, N_HEADS), jnp.int32),
        compiler_params=_params(),
        name="moba_select_sample",
    )(gates)


def _moba_sample_kernel(phys_ref, tile_ref, ck_ref, cv_ref, q_ref, kn_ref, vn_ref, bt_ref, bias0_ref,
                        o_ref, kbuf_ref, vbuf_ref, sem_ref, *, n_sel):
    b = pl.program_id(0)
    n_b = pl.num_programs(0)
    ppb = MOBA_BLOCK // PAGE_SIZE
    n_pg = n_sel * ppb

    def copies(bi, slot):
        out = []
        for h in range(N_HEADS):
            g = h // GROUP
            for p in range(n_pg):
                page = phys_ref[bi, h * n_pg + p]
                out.append(pltpu.make_async_copy(ck_ref.at[page, g], kbuf_ref.at[slot, h, p],
                                                 sem_ref.at[0, slot]))
                out.append(pltpu.make_async_copy(cv_ref.at[page, g], vbuf_ref.at[slot, h, p],
                                                 sem_ref.at[1, slot]))
        return out

    @pl.when(b == 0)
    def _():
        for cp in copies(b, 0):
            cp.start()

    @pl.when(b + 1 < n_b)
    def _():
        for cp in copies(b + 1, (b + 1) % 2):
            cp.start()

    slot = b % 2
    for cp in copies(b, slot):
        cp.wait()

    rnd = lambda x: x.astype(BF16).astype(F32)
    for h in range(N_HEADS):
        g = h // GROUP
        q = rnd(q_ref[0, :, h:h + 1] * ATTN_SCALE)
        s = jnp.concatenate([jnp.sum(rnd(kbuf_ref[slot, h, p]) * q, axis=0, keepdims=True)
                             for p in range(n_pg)], axis=0)
        bias = jnp.concatenate([bt_ref[tile_ref[b, h * n_sel + si], h] for si in range(n_sel)], axis=0)
        s = s + bias
        s0 = (jnp.sum(q * rnd(kn_ref[0, :, g:g + 1]), axis=0, keepdims=True)
              + bias0_ref[:, h:h + 1])
        m = jnp.maximum(jnp.max(s, keepdims=True), s0)
        e = jnp.exp(s - m)
        e0 = jnp.exp(s0 - m)
        inv = 1.0 / (jnp.sum(e, keepdims=True) + e0)
        p = rnd(e * inv)
        acc = rnd(vbuf_ref[slot, h, 0]) * p[0:1, :]
        for pg in range(1, n_pg):
            acc = acc + rnd(vbuf_ref[slot, h, pg]) * p[pg:pg + 1, :]
        o = jnp.sum(acc, axis=1, keepdims=True) + rnd(e0 * inv) * rnd(vn_ref[0, :, g:g + 1])
        o_ref[0, :, h:h + 1] = o


def moba_attn_sample(phys, tiles, cache_kt, cache_vt, q_cols, k_new, v_new, bias_tiles, bias0, n_sel):
    db = q_cols.shape[0]
    n_pg = n_sel * (MOBA_BLOCK // PAGE_SIZE)
    row = lambda b, *_: (b, 0, 0)
    return pl.pallas_call(
        functools.partial(_moba_sample_kernel, n_sel=n_sel),
        grid_spec=pltpu.PrefetchScalarGridSpec(
            num_scalar_prefetch=2,
            grid=(db,),
            in_specs=[pl.BlockSpec(memory_space=pl.ANY),
                      pl.BlockSpec(memory_space=pl.ANY),
                      pl.BlockSpec((1, HEAD_DIM, N_HEADS), row),
                      pl.BlockSpec((1, HEAD_DIM, N_KV_HEADS), row),
                      pl.BlockSpec((1, HEAD_DIM, N_KV_HEADS), row),
                      pl.BlockSpec(bias_tiles.shape, lambda b, *_: (0, 0, 0, 0)),
                      pl.BlockSpec((1, N_HEADS), lambda b, *_: (0, 0))],
            out_specs=pl.BlockSpec((1, HEAD_DIM, N_HEADS), row),
            scratch_shapes=[pltpu.VMEM((2, N_HEADS, n_pg, HEAD_DIM, PAGE_SIZE), F32),
                            pltpu.VMEM((2, N_HEADS, n_pg, HEAD_DIM, PAGE_SIZE), F32),
                            pltpu.SemaphoreType.DMA((2, 2))]),
        out_shape=jax.ShapeDtypeStruct((db, HEAD_DIM, N_HEADS), F32),
        compiler_params=_params(dimension_semantics=("arbitrary",)),
        name="moba_attn_sample",
    )(phys, tiles, cache_kt, cache_vt, q_cols, k_new, v_new, bias_tiles, bias0)


def _router_kernel(x_ref, g_ref, wr_ref, xn_ref, gate_ref, idx_ref):
    xn = _rms(x_ref[...], g_ref[...])
    xn_ref[...] = xn
    logits = jnp.dot(xn.astype(BF16), wr_ref[...], preferred_element_type=F32)
    n_e = logits.shape[1]
    col = lax.broadcasted_iota(jnp.int32, logits.shape, 1).astype(F32)
    m1 = jnp.max(logits, axis=1, keepdims=True)
    i1 = jnp.min(jnp.where(logits == m1, col, float(n_e)), axis=1, keepdims=True)
    rest = jnp.where(col == i1, -jnp.inf, logits)
    m2 = jnp.max(rest, axis=1, keepdims=True)
    i2 = jnp.min(jnp.where(rest == m2, col, float(n_e)), axis=1, keepdims=True)
    e2 = jnp.exp(m2 - m1)
    den = 1.0 + e2
    gate_ref[...] = jnp.concatenate([1.0 / den, e2 / den], axis=1)
    idx_ref[...] = jnp.concatenate([i1, i2], axis=1).astype(jnp.int32)


def moe_router(x, g, w_router, tm):
    m, d = x.shape
    n_e = w_router.shape[1]
    return pl.pallas_call(
        _router_kernel,
        grid=(m // tm,),
        in_specs=[pl.BlockSpec((tm, d), lambda i: (i, 0)),
                  pl.BlockSpec((1, d), lambda i: (0, 0)),
                  pl.BlockSpec((d, n_e), lambda i: (0, 0))],
        out_specs=[pl.BlockSpec((tm, d), lambda i: (i, 0)),
                   pl.BlockSpec((tm, EXPERT_TOPK), lambda i: (i, 0)),
                   pl.BlockSpec((tm, EXPERT_TOPK), lambda i: (i, 0))],
        out_shape=[jax.ShapeDtypeStruct((m, d), F32),
                   jax.ShapeDtypeStruct((m, EXPERT_TOPK), F32),
                   jax.ShapeDtypeStruct((m, EXPERT_TOPK), jnp.int32)],
        compiler_params=_params(),
        name="moe_router",
    )(x, g.reshape(1, d), w_router)


def _moe_ffn_kernel(te_ref, live_ref, x_ref, wg_ref, wu_ref, wd_ref, o_ref, xb_ref, acc_ref):
    i = pl.program_id(0)
    j = pl.program_id(1)

    @pl.when(j == 0)
    def _():
        xb_ref[...] = x_ref[...].astype(BF16)
        acc_ref[...] = jnp.zeros_like(acc_ref)

    @pl.when(live_ref[i] > 0)
    def _():
        x = xb_ref[...]
        a = jnp.dot(x, wg_ref[0], preferred_element_type=F32)
        b = jnp.dot(x, wu_ref[0], preferred_element_type=F32)
        h = (a * jax.nn.sigmoid(a)) * b
        acc_ref[...] += jnp.dot(h.astype(BF16), wd_ref[0], preferred_element_type=F32)

    @pl.when(j == pl.num_programs(1) - 1)
    def _():
        o_ref[...] = acc_ref[...]


def moe_ffn(tile_expert, tile_live, x_sorted, wg, wu, wd, tm, tn):
    p, d = x_sorted.shape
    f = wg.shape[2]
    return pl.pallas_call(
        _moe_ffn_kernel,
        grid_spec=pltpu.PrefetchScalarGridSpec(
            num_scalar_prefetch=2,
            grid=(p // tm, f // tn),
            in_specs=[pl.BlockSpec((tm, d), lambda i, j, te, lv: (i, 0)),
                      pl.BlockSpec((1, d, tn), lambda i, j, te, lv: (te[i], 0, j)),
                      pl.BlockSpec((1, d, tn), lambda i, j, te, lv: (te[i], 0, j)),
                      pl.BlockSpec((1, tn, d), lambda i, j, te, lv: (te[i], j, 0))],
            out_specs=pl.BlockSpec((tm, d), lambda i, j, te, lv: (i, 0)),
            scratch_shapes=[pltpu.VMEM((tm, d), BF16), pltpu.VMEM((tm, d), F32)]),
        out_shape=jax.ShapeDtypeStruct((p, d), F32),
        compiler_params=_params(),
        name="moe_ffn",
    )(tile_expert, tile_live, x_sorted, wg, wu, wd)


def _combine_norm_kernel(y_ref, e1_ref, e2_ref, gate_ref, g_ref, o_ref):
    gate = gate_ref[...]
    y = y_ref[...] + (gate[:, 0:1] * e1_ref[...] + gate[:, 1:2] * e2_ref[...])
    o_ref[...] = _rms(y, g_ref[...])


def combine_norm(y, e1, e2, gates, g, tm):
    m, d = y.shape
    blk = pl.BlockSpec((tm, d), lambda i: (i, 0))
    return pl.pallas_call(
        _combine_norm_kernel,
        grid=(m // tm,),
        in_specs=[blk, blk, blk,
                  pl.BlockSpec((tm, EXPERT_TOPK), lambda i: (i, 0)),
                  pl.BlockSpec((1, d), lambda i: (0, 0))],
        out_specs=blk,
        out_shape=jax.ShapeDtypeStruct((m, d), F32),
        compiler_params=_params(),
        name="combine_norm",
    )(y, e1, e2, gates, g.reshape(1, d))


def moe_dispatch_plan(idx, tm):
    m, k = idx.shape
    n_asg = m * k
    p = -(-(n_asg + N_EXPERTS * (tm - 1)) // tm) * tm
    flat = idx.reshape(n_asg)
    onehot = (flat[:, None] == jnp.arange(N_EXPERTS, dtype=jnp.int32)[None, :]).astype(jnp.int32)
    csum = jnp.cumsum(onehot, axis=0)
    count = csum[-1]
    rank = jnp.take_along_axis(csum, flat[:, None], axis=1)[:, 0] - 1
    padded = -(-count // tm) * tm
    pad_end = jnp.cumsum(padded)
    pad_start = pad_end - padded
    slot = pad_start[flat] + rank
    order = jnp.sort(flat * n_asg + jnp.arange(n_asg, dtype=jnp.int32)) % n_asg
    start = jnp.cumsum(count) - count
    slots = jnp.arange(p, dtype=jnp.int32)
    slot_e = jnp.minimum(jnp.searchsorted(pad_end, slots, side="right"), N_EXPERTS - 1).astype(jnp.int32)
    r = slots - pad_start[slot_e]
    live = r < count[slot_e]
    src_asg = order[jnp.clip(start[slot_e] + r, 0, n_asg - 1)]
    row_src = jnp.where(live, src_asg // k, 0).astype(jnp.int32)
    tile_expert = slot_e[::tm]
    tile_live = live[::tm].astype(jnp.int32)
    return row_src, slot.reshape(m, k), tile_expert, tile_live


def _window_prompt_bias(table):
    kk = np.arange(2 * BAND_BLOCK)[:, None]
    i = np.arange(BAND_BLOCK)[None, :]
    dist = BAND_BLOCK + i - kk
    valid = (dist >= 0) & (dist <= WINDOW)
    bias = table[_bucket_np(dist)]
    bias = jnp.where(valid[..., None], bias, NEG_INF)
    bias = bias.reshape(2 * BAND_BLOCK, BAND_BLOCK, N_KV_HEADS, GROUP).transpose(2, 0, 3, 1)
    return bias.reshape(N_KV_HEADS, 2 * BAND_BLOCK, QCOLS)


def _moba_prompt_bias(table):
    u = np.arange(MOBA_FAR_TILE + 1)[:, None, None]
    kk = np.arange(MOBA_BLOCK)[None, :, None]
    i = np.arange(MOBA_Q_BLOCK)[None, None, :]
    dist = u * MOBA_Q_BLOCK + i - kk
    dist = np.where(u == MOBA_FAR_TILE, np.maximum(dist, FAR_DIST), dist)
    bias = table[_bucket_np(dist)]
    bias = jnp.where((dist >= 0)[..., None], bias, NEG_INF)
    n_t = MOBA_FAR_TILE + 1
    bias = bias.reshape(n_t, MOBA_BLOCK, MOBA_Q_BLOCK, N_KV_HEADS, GROUP).transpose(3, 0, 1, 4, 2)
    return bias.reshape(N_KV_HEADS, n_t, MOBA_BLOCK, QCOLS)


def _moba_sample_bias(table, past_len):
    ti = np.arange(SAMPLE_FAR_TILE + 1)[:, None]
    r = np.arange(MOBA_BLOCK)[None, :]
    dist = (ti + 1) * MOBA_BLOCK - r + (past_len % MOBA_BLOCK)
    dist = np.where(ti == SAMPLE_FAR_TILE, np.maximum(dist, FAR_DIST), dist)
    bias = table[_bucket_np(dist)].transpose(0, 2, 1)
    return bias.reshape(SAMPLE_FAR_TILE + 1, N_HEADS, MOBA_BLOCK // PAGE_SIZE, PAGE_SIZE)


def kernel(x_prompt, x_sample, state_win_k, state_win_v, cache_k, cache_v, page_table, rel_bias_table,
           attn_norm_g, ffn_norm_g, w_qkv_a, sink_a, w_o_a, kv_norm_g, w_kv_shared, w_q_b, w_o_b,
           w_gate_dense, w_up_dense, w_down_dense, w_router, w_gate_moe, w_up_moe, w_down_moe, final_norm_g):
    bp, sp, d = x_prompt.shape
    bs, ss, _ = x_sample.shape
    assert ss == 1 and attn_norm_g.shape[0] == 2
    mp = bp * sp
    hq = N_HEADS * HEAD_DIM
    hk = N_KV_HEADS * HEAD_DIM
    past_len = page_table.shape[1] * PAGE_SIZE
    assert past_len % MOBA_BLOCK == 0 and sp % MOBA_BLOCK == 0
    n_full = past_len // MOBA_BLOCK
    n_sel = min(MOBA_TOPK, n_full)
    w_buf = state_win_k.shape[2]
    table = rel_bias_table.astype(F32)

    tm_p = _tile(mp, 512)
    tm_s = _tile(bs, 512)
    tn = _tile(w_gate_dense.shape[2], 896) if w_gate_dense.shape[2] % 128 == 0 else w_gate_dense.shape[2]
    cast = lambda w: w.astype(BF16)

    yp = x_prompt.reshape(mp, d)
    ys = x_sample.reshape(bs, d)

    w_qkv = cast(w_qkv_a[0])
    qkv_p = rms_matmul(yp, attn_norm_g[0], w_qkv, tm_p)
    qkv_s = rms_matmul(ys, attn_norm_g[0], w_qkv, tm_s)

    def prompt_layouts(q, k, v):
        qt = q.reshape(bp, sp, N_HEADS, HEAD_DIM).transpose(0, 2, 3, 1)
        k4 = k.reshape(bp, sp, N_KV_HEADS, HEAD_DIM).transpose(0, 2, 1, 3)
        vt = v.reshape(bp, sp, N_KV_HEADS, HEAD_DIM).transpose(0, 2, 3, 1)
        return qt, k4, vt

    def from_heads_t(ot):
        return ot.transpose(0, 3, 1, 2).reshape(mp, hq)

    kp = qkv_p[:, hq:hq + hk]
    vp = qkv_p[:, hq + hk:]
    qt, k4, vt = prompt_layouts(qkv_p[:, :hq], kp, vp)
    sink_cols = jnp.repeat(sink_a[0].astype(F32).reshape(N_KV_HEADS, 1, GROUP), BAND_BLOCK, axis=2)
    mix_p = from_heads_t(window_attn_prompt(qt, k4, vt, _window_prompt_bias(table), sink_cols))

    head_mask = jnp.asarray((np.arange(N_HEADS)[:, None] // GROUP == np.arange(hk)[None, :] // HEAD_DIM)
                            .astype(np.float32))
    ks_new = qkv_s[:, hq:hq + hk]
    vs_new = qkv_s[:, hq + hk:]
    k_buf = state_win_k[:, 0].reshape(bs, w_buf, hk)
    v_buf = state_win_v[:, 0].reshape(bs, w_buf, hk)
    bias_ws = table[_bucket_np(w_buf - np.arange(w_buf))].T
    bias_0 = table[0][:, None]
    mix_s = window_attn_sample(qkv_s[:, :hq].reshape(bs, N_HEADS, HEAD_DIM), ks_new[:, None], vs_new[:, None],
                               k_buf, v_buf, bias_ws, bias_0, sink_a[0].astype(F32)[:, None], head_mask)

    w_o = cast(w_o_a[0])
    yp = matmul_res(mix_p, w_o, yp, tm_p)
    ys = matmul_res(mix_s.reshape(bs, hq), w_o, ys, tm_s)

    wg, wu, wd = cast(w_gate_dense[0]), cast(w_up_dense[0]), cast(w_down_dense[0])
    yp = ffn_dense(yp, ffn_norm_g[0], wg, wu, wd, tm_p, tn)
    ys = ffn_dense(ys, ffn_norm_g[0], wg, wu, wd, tm_s, tn)

    w_kv = cast(w_kv_shared)
    kv_p = rms_matmul(yp, kv_norm_g, w_kv, tm_p)
    kv_s = rms_matmul(ys, kv_norm_g, w_kv, tm_s)
    w_q = cast(w_q_b[0])
    q_p = rms_matmul(yp, attn_norm_g[1], w_q, tm_p)
    q_s = rms_matmul(ys, attn_norm_g[1], w_q, tm_s)

    qt, k4, vt = prompt_layouts(q_p, kv_p[:, :hk], kv_p[:, hk:])
    mix_p = from_heads_t(moba_attn_prompt(qt, k4, vt, _moba_prompt_bias(table)))

    cache_kt = cache_k.transpose(0, 2, 3, 1)
    cache_vt = cache_v.transpose(0, 2, 3, 1)
    q_cols = q_s.reshape(bs, N_HEADS, HEAD_DIM).transpose(0, 2, 1)
    group_mask = jnp.asarray((np.arange(N_KV_HEADS)[:, None, None] == np.arange(N_HEADS)[None, None, :] // GROUP)
                             .astype(np.float32))
    gates_s = page_block_gates(page_table, cache_kt, q_cols[:, None] * group_mask, n_full)
    sel = moba_select_sample(gates_s, n_sel).transpose(0, 2, 1)
    ppb = MOBA_BLOCK // PAGE_SIZE
    pages = sel[..., None] * ppb + jnp.arange(ppb, dtype=jnp.int32)
    phys = jnp.take_along_axis(page_table, pages.reshape(bs, -1), axis=1)
    tiles = jnp.clip(n_full - 1 - sel, 0, SAMPLE_FAR_TILE).reshape(bs, -1)
    to_cols = lambda t: t.reshape(bs, N_KV_HEADS, HEAD_DIM).transpose(0, 2, 1)
    o_cols = moba_attn_sample(phys, tiles, cache_kt, cache_vt, q_cols, to_cols(kv_s[:, :hk]),
                              to_cols(kv_s[:, hk:]), _moba_sample_bias(table, past_len), table[0][None, :],
                              n_sel)
    mix_s = o_cols.transpose(0, 2, 1)

    w_o = cast(w_o_b[0])
    yp = matmul_res(mix_p, w_o, yp, tm_p)
    ys = matmul_res(mix_s.reshape(bs, hq), w_o, ys, tm_s)

    y_all = jnp.concatenate([yp, ys], axis=0)
    m_all = mp + bs
    tm_a = _tile(m_all, 512)
    xn, gates, idx = moe_router(y_all, ffn_norm_g[1], cast(w_router[0]), tm_a)
    tm_e = 512
    row_src, slot, tile_expert, tile_live = moe_dispatch_plan(idx, tm_e)
    x_sorted = jnp.take(xn, row_src, axis=0)
    e_sorted = moe_ffn(tile_expert, tile_live, x_sorted, cast(w_gate_moe[0]), cast(w_up_moe[0]),
                       cast(w_down_moe[0]), tm_e, tn)
    e1 = jnp.take(e_sorted, slot[:, 0], axis=0)
    e2 = jnp.take(e_sorted, slot[:, 1], axis=0)
    y_out = combine_norm(y_all, e1, e2, gates, final_norm_g, tm_a)

    y_prompt = y_out[:mp].reshape(bp, sp, d)
    y_sample = y_out[mp:].reshape(bs, ss, d)
    w_keep = min(WINDOW, sp)
    kv4 = lambda t, b, s: t.reshape(b, s, N_KV_HEADS, HEAD_DIM)
    win_k_prompt = kv4(kp, bp, sp)[:, None, sp - w_keep:]
    win_v_prompt = kv4(vp, bp, sp)[:, None, sp - w_keep:]
    win_k_sample = jnp.concatenate([state_win_k[:, 0], kv4(ks_new, bs, 1)], axis=1)[:, None, ss:]
    win_v_sample = jnp.concatenate([state_win_v[:, 0], kv4(vs_new, bs, 1)], axis=1)[:, None, ss:]
    return (y_prompt, y_sample, win_k_prompt, win_v_prompt,
            kv4(kv_p[:, :hk], bp, sp), kv4(kv_p[:, hk:], bp, sp),
            win_k_sample, win_v_sample,
            kv4(kv_s[:, :hk], bs, 1), kv4(kv_s[:, hk:], bs, 1))
```

```python
import functools
import math

import numpy as np
import jax
import jax.numpy as jnp
from jax import lax
from jax.experimental import pallas as pl
from jax.experimental.pallas import tpu as pltpu

N_HEADS = 16
N_KV_HEADS = 4
HEAD_DIM = 64
GROUP = N_HEADS // N_KV_HEADS
WINDOW = 128
BAND_BLOCK = 128
MOBA_BLOCK = 256
MOBA_TOPK = 3
MOBA_Q_BLOCK = 128
PAGE_SIZE = 128
N_BUCKETS = 32
REL_MAX_DIST = 1024
N_EXPERTS = 8
EXPERT_TOPK = 2
RMS_EPS = 1e-5
NEG_INF = -1e30
ATTN_SCALE = HEAD_DIM ** -0.5
QCOLS = GROUP * MOBA_Q_BLOCK
FAR_DIST = 790
MOBA_FAR_TILE = 9
SAMPLE_FAR_TILE = 4
SUBLANES = 8

F32 = jnp.float32
BF16 = jnp.bfloat16
VMEM_LIMIT = 56 * 1024 * 1024


def _bucket_np(dist):
    max_exact = N_BUCKETS // 2
    n = np.maximum(np.asarray(dist, np.int64), 0)
    nf = np.maximum(n, 1).astype(np.float64)
    large = max_exact + (np.log(nf / max_exact) / math.log(REL_MAX_DIST / max_exact)
                         * (N_BUCKETS - max_exact)).astype(np.int64)
    return np.where(n < max_exact, n, np.minimum(large, N_BUCKETS - 1)).astype(np.int32)


def _tile(n, pref):
    if n <= pref:
        return n
    for t in range(pref, 7, -8):
        if n % t == 0:
            return t
    return n


def _params(**kw):
    return pltpu.CompilerParams(vmem_limit_bytes=VMEM_LIMIT, **kw)


def _rms(x, g):
    return (x * lax.rsqrt(jnp.mean(x * x, axis=-1, keepdims=True) + RMS_EPS)) * g


def _rms_matmul_kernel(x_ref, g_ref, w_ref, o_ref):
    xn = _rms(x_ref[...], g_ref[...]).astype(BF16)
    o_ref[...] = jnp.dot(xn, w_ref[...], preferred_element_type=F32)


def rms_matmul(x, g, w, tm):
    m, d = x.shape
    n = w.shape[1]
    return pl.pallas_call(
        _rms_matmul_kernel,
        grid=(m // tm,),
        in_specs=[pl.BlockSpec((tm, d), lambda i: (i, 0)),
                  pl.BlockSpec((1, d), lambda i: (0, 0)),
                  pl.BlockSpec((d, n), lambda i: (0, 0))],
        out_specs=pl.BlockSpec((tm, n), lambda i: (i, 0)),
        out_shape=jax.ShapeDtypeStruct((m, n), F32),
        compiler_params=_params(),
        name="rms_matmul",
    )(x, g.reshape(1, d), w)


def _matmul_res_kernel(a_ref, w_ref, r_ref, o_ref):
    o_ref[...] = r_ref[...] + jnp.dot(a_ref[...].astype(BF16), w_ref[...], preferred_element_type=F32)


def matmul_res(a, w, res, tm):
    m, k = a.shape
    n = w.shape[1]
    return pl.pallas_call(
        _matmul_res_kernel,
        grid=(m // tm,),
        in_specs=[pl.BlockSpec((tm, k), lambda i: (i, 0)),
                  pl.BlockSpec((k, n), lambda i: (0, 0)),
                  pl.BlockSpec((tm, n), lambda i: (i, 0))],
        out_specs=pl.BlockSpec((tm, n), lambda i: (i, 0)),
        out_shape=jax.ShapeDtypeStruct((m, n), F32),
        compiler_params=_params(),
        name="matmul_res",
    )(a, w, res)


def _ffn_kernel(x_ref, g_ref, wg_ref, wu_ref, wd_ref, o_ref, xn_ref, acc_ref):
    j = pl.program_id(1)

    @pl.when(j == 0)
    def _():
        xn_ref[...] = _rms(x_ref[...], g_ref[...]).astype(BF16)
        acc_ref[...] = jnp.zeros_like(acc_ref)

    xn = xn_ref[...]
    a = jnp.dot(xn, wg_ref[...], preferred_element_type=F32)
    b = jnp.dot(xn, wu_ref[...], preferred_element_type=F32)
    h = (a * jax.nn.sigmoid(a)) * b
    acc_ref[...] += jnp.dot(h.astype(BF16), wd_ref[...], preferred_element_type=F32)

    @pl.when(j == pl.num_programs(1) - 1)
    def _():
        o_ref[...] = x_ref[...] + acc_ref[...]


def ffn_dense(x, g, wg, wu, wd, tm, tn):
    m, d = x.shape
    f = wg.shape[1]
    return pl.pallas_call(
        _ffn_kernel,
        grid=(m // tm, f // tn),
        in_specs=[pl.BlockSpec((tm, d), lambda i, j: (i, 0)),
                  pl.BlockSpec((1, d), lambda i, j: (0, 0)),
                  pl.BlockSpec((d, tn), lambda i, j: (0, j)),
                  pl.BlockSpec((d, tn), lambda i, j: (0, j)),
                  pl.BlockSpec((tn, d), lambda i, j: (j, 0))],
        out_specs=pl.BlockSpec((tm, d), lambda i, j: (i, 0)),
        out_shape=jax.ShapeDtypeStruct((m, d), F32),
        scratch_shapes=[pltpu.VMEM((tm, d), BF16), pltpu.VMEM((tm, d), F32)],
        compiler_params=_params(),
        name="ffn_dense",
    )(x, g.reshape(1, d), wg, wu, wd)


def _stack_heads(q_ref, g=0):
    return jnp.concatenate([q_ref[0, g * GROUP + r] for r in range(GROUP)], axis=1)


def _unstack_heads(o_ref, o, g=0):
    for r in range(GROUP):
        o_ref[0, g * GROUP + r] = o[:, r * MOBA_Q_BLOCK:(r + 1) * MOBA_Q_BLOCK].astype(o_ref.dtype)


def _win_prompt_kernel(q_ref, kp_ref, ko_ref, vp_ref, vo_ref, wb_ref, sink_ref, o_ref):
    n = pl.program_id(1)
    row = lax.broadcasted_iota(jnp.int32, (2 * BAND_BLOCK, QCOLS), 0)
    has_prev = (row >= BAND_BLOCK) | (n > 0)
    for g in range(N_KV_HEADS):
        q = (_stack_heads(q_ref, g) * ATTN_SCALE).astype(BF16)
        k = jnp.concatenate([kp_ref[0, g], ko_ref[0, g]], axis=0).astype(BF16)
        s = jnp.dot(k, q, preferred_element_type=F32) + wb_ref[g]
        s = jnp.where(has_prev, s, NEG_INF)
        sink = sink_ref[g]
        m = jnp.maximum(jnp.max(s, axis=0, keepdims=True), sink)
        e = jnp.exp(s - m)
        den = jnp.sum(e, axis=0, keepdims=True) + jnp.exp(sink - m)
        vt = jnp.concatenate([vp_ref[0, g], vo_ref[0, g]], axis=1).astype(BF16)
        o = jnp.dot(vt, e.astype(BF16), preferred_element_type=F32) / den
        _unstack_heads(o_ref, o, g)


def window_attn_prompt(qt, k4, vt, wb, sink_cols):
    b, _, _, s = qt.shape
    nb = s // BAND_BLOCK
    qb = BAND_BLOCK
    prev = lambda n: jnp.maximum(n - 1, 0)
    return pl.pallas_call(
        _win_prompt_kernel,
        grid=(b, nb),
        in_specs=[pl.BlockSpec((1, N_HEADS, HEAD_DIM, qb), lambda b, n: (b, 0, 0, n)),
                  pl.BlockSpec((1, N_KV_HEADS, qb, HEAD_DIM), lambda b, n: (b, 0, prev(n), 0)),
                  pl.BlockSpec((1, N_KV_HEADS, qb, HEAD_DIM), lambda b, n: (b, 0, n, 0)),
                  pl.BlockSpec((1, N_KV_HEADS, HEAD_DIM, qb), lambda b, n: (b, 0, 0, prev(n))),
                  pl.BlockSpec((1, N_KV_HEADS, HEAD_DIM, qb), lambda b, n: (b, 0, 0, n)),
                  pl.BlockSpec((N_KV_HEADS, 2 * qb, QCOLS), lambda b, n: (0, 0, 0)),
                  pl.BlockSpec((N_KV_HEADS, 1, QCOLS), lambda b, n: (0, 0, 0))],
        out_specs=pl.BlockSpec((1, N_HEADS, HEAD_DIM, qb), lambda b, n: (b, 0, 0, n)),
        out_shape=jax.ShapeDtypeStruct(qt.shape, BF16),
        compiler_params=_params(),
        name="window_attn_prompt",
    )(qt, k4, k4, vt, vt, wb, sink_cols)


def _moba_prompt_kernel(q_ref, k_ref, vt_ref, tb_ref, o_ref, kmean_ref, sel_ref, *, n_blk):
    t = pl.program_id(2)
    qb = t // (MOBA_BLOCK // MOBA_Q_BLOCK)

    @pl.when(t == 0)
    def _():
        kf = k_ref[0, 0].reshape(n_blk, MOBA_BLOCK, HEAD_DIM)
        kmean_ref[...] = jnp.mean(kf, axis=1)

    q32 = _stack_heads(q_ref)
    gate = jnp.dot(kmean_ref[...].astype(BF16), q32.astype(BF16),
                   preferred_element_type=F32)
    blk = lax.broadcasted_iota(jnp.int32, gate.shape, 0)
    past = blk < qb
    gate = jnp.where(past, gate, NEG_INF)
    rank = jnp.zeros(gate.shape, jnp.int32)
    for kk in range(n_blk):
        gk = gate[kk:kk + 1, :]
        beats = (gk > gate) | ((gk == gate) & (kk < blk))
        rank = rank + beats.astype(jnp.int32)
    sel = (past & (rank < MOBA_TOPK)) | (blk == qb)
    sel_ref[...] = sel.astype(F32)
    qs = (q32 * ATTN_SCALE).astype(BF16)

    tiles_per_blk = MOBA_BLOCK // MOBA_Q_BLOCK

    def masked_logits(s, j):
        u = jnp.clip(t - tiles_per_blk * j, 0, MOBA_FAR_TILE)
        return jnp.where(sel_ref[pl.ds(j, 1), :] > 0, s + tb_ref[0, u], NEG_INF)

    def body(jp, carry):
        m, l, acc = carry
        off = pl.multiple_of(jp * (2 * MOBA_BLOCK), 2 * MOBA_BLOCK)
        kj = k_ref[0, 0, pl.ds(off, 2 * MOBA_BLOCK), :].astype(BF16)
        s = jnp.dot(kj, qs, preferred_element_type=F32)
        s0 = masked_logits(s[:MOBA_BLOCK], 2 * jp)
        s1 = masked_logits(s[MOBA_BLOCK:], 2 * jp + 1)
        m_new = jnp.maximum(m, jnp.maximum(jnp.max(s0, axis=0, keepdims=True),
                                           jnp.max(s1, axis=0, keepdims=True)))
        alpha = jnp.exp(m - m_new)
        p0 = jnp.exp(s0 - m_new)
        p1 = jnp.exp(s1 - m_new)
        l = alpha * l + (jnp.sum(p0, axis=0, keepdims=True) + jnp.sum(p1, axis=0, keepdims=True))
        vj = vt_ref[0, 0, :, pl.ds(off, 2 * MOBA_BLOCK)].astype(BF16)
        p = jnp.concatenate([p0, p1], axis=0).astype(BF16)
        acc = alpha * acc + jnp.dot(vj, p, preferred_element_type=F32)
        return m_new, l, acc

    init = (jnp.full((1, QCOLS), NEG_INF, F32), jnp.zeros((1, QCOLS), F32),
            jnp.zeros((HEAD_DIM, QCOLS), F32))
    m, l, acc = lax.fori_loop(0, qb // 2 + 1, body, init)
    _unstack_heads(o_ref, acc / l)


def moba_attn_prompt(qt, k4, vt, tb):
    b, _, _, s = qt.shape
    n_blk = s // MOBA_BLOCK
    assert n_blk % 2 == 0, "kv blocks are swept in pairs"
    nt = s // MOBA_Q_BLOCK
    n_tb = tb.shape[1]
    return pl.pallas_call(
        functools.partial(_moba_prompt_kernel, n_blk=n_blk),
        grid=(N_KV_HEADS, b, nt),
        in_specs=[pl.BlockSpec((1, GROUP, HEAD_DIM, MOBA_Q_BLOCK), lambda g, b, t: (b, g, 0, t)),
                  pl.BlockSpec((1, 1, s, HEAD_DIM), lambda g, b, t: (b, g, 0, 0)),
                  pl.BlockSpec((1, 1, HEAD_DIM, s), lambda g, b, t: (b, g, 0, 0)),
                  pl.BlockSpec((1, n_tb, MOBA_BLOCK, QCOLS), lambda g, b, t: (g, 0, 0, 0))],
        out_specs=pl.BlockSpec((1, GROUP, HEAD_DIM, MOBA_Q_BLOCK), lambda g, b, t: (b, g, 0, t)),
        out_shape=jax.ShapeDtypeStruct(qt.shape, BF16),
        scratch_shapes=[pltpu.VMEM((n_blk, HEAD_DIM), F32), pltpu.VMEM((n_blk, QCOLS), F32)],
        compiler_params=_params(),
        name="moba_attn_prompt",
    )(qt, k4, vt, tb)


def _expand_heads(q, mask):
    return jnp.concatenate([q] * N_KV_HEADS, axis=1) * mask


def _fold_heads(o):
    acc = o
    for c in range(1, N_KV_HEADS):
        acc = acc + pltpu.roll(o, c * HEAD_DIM, axis=1)
    return acc[:, :HEAD_DIM]


def _win_sample_kernel(q_ref, kn_ref, vn_ref, kb_ref, vb_ref, bias_ref, bias0_ref, sink_ref,
                       mask_ref, o_ref, *, bb):
    mask = mask_ref[...]
    sink = sink_ref[...]
    rnd = lambda x: x.astype(BF16).astype(F32)
    for i in range(bb):
        qe = _expand_heads(q_ref[i] * ATTN_SCALE, mask).astype(BF16)
        s = lax.dot_general(qe, kb_ref[i].astype(BF16), (((1,), (1,)), ((), ())),
                            preferred_element_type=F32) + bias_ref[...]
        s0 = jnp.sum(qe.astype(F32) * rnd(kn_ref[i]), axis=1, keepdims=True) + bias0_ref[...]
        m = jnp.maximum(jnp.maximum(jnp.max(s, axis=1, keepdims=True), s0), sink)
        e = jnp.exp(s - m)
        e0 = jnp.exp(s0 - m)
        inv = 1.0 / (jnp.sum(e, axis=1, keepdims=True) + e0 + jnp.exp(sink - m))
        o = (jnp.dot((e * inv).astype(BF16), vb_ref[i].astype(BF16), preferred_element_type=F32)
             + rnd(e0 * inv) * rnd(vn_ref[i]))
        o_ref[i] = _fold_heads(o * mask)


def window_attn_sample(q3, k_new, v_new, k_buf, v_buf, bias, bias0, sink, mask):
    db, w, kvw = k_buf.shape
    bb = _tile(db, 8)
    row = lambda i: (i, 0, 0)
    fixed = lambda i: (0, 0)
    return pl.pallas_call(
        functools.partial(_win_sample_kernel, bb=bb),
        grid=(db // bb,),
        in_specs=[pl.BlockSpec((bb, N_HEADS, HEAD_DIM), row),
                  pl.BlockSpec((bb, 1, kvw), row),
                  pl.BlockSpec((bb, 1, kvw), row),
                  pl.BlockSpec((bb, w, kvw), row),
                  pl.BlockSpec((bb, w, kvw), row),
                  pl.BlockSpec((N_HEADS, w), fixed),
                  pl.BlockSpec((N_HEADS, 1), fixed),
                  pl.BlockSpec((N_HEADS, 1), fixed),
                  pl.BlockSpec((N_HEADS, kvw), fixed)],
        out_specs=pl.BlockSpec((bb, N_HEADS, HEAD_DIM), row),
        out_shape=jax.ShapeDtypeStruct((db, N_HEADS, HEAD_DIM), F32),
        compiler_params=_params(),
        name="window_attn_sample",
    )(q3, k_new, v_new, k_buf, v_buf, bias, bias0, sink, mask)


def _copy_page(cache_ref, page, dst_ref, sem):
    return pltpu.make_async_copy(cache_ref.at[page], dst_ref, sem)


def _page_gate_kernel(pt_ref, cache_ref, qm_ref, o_ref, buf_ref, sem_ref, *, cpp, chunks_per_b):
    step = pl.program_id(0)
    n_steps = pl.num_programs(0)

    def copies(s, slot):
        b = s // chunks_per_b
        c = s % chunks_per_b
        return [_copy_page(cache_ref, pt_ref[b, c * cpp + p], buf_ref.at[slot, p], sem_ref.at[slot])
                for p in range(cpp)]

    @pl.when(step == 0)
    def _():
        for cp in copies(step, 0):
            cp.start()

    @pl.when(step + 1 < n_steps)
    def _():
        for cp in copies(step + 1, (step + 1) % 2):
            cp.start()

    slot = step % 2
    for cp in copies(step, slot):
        cp.wait()
    ppb = MOBA_BLOCK // PAGE_SIZE
    qm = qm_ref[0].astype(BF16).astype(F32)
    for n in range(cpp // ppb):
        x = jnp.sum(buf_ref[slot, pl.ds(n * ppb, ppb)], axis=0)
        mean = jnp.sum(x, axis=-1, keepdims=True) * (1.0 / MOBA_BLOCK)
        mean = mean.astype(BF16).astype(F32)
        o_ref[0, n:n + 1, :] = jnp.sum(jnp.sum(mean * qm, axis=0), axis=0, keepdims=True)


def page_block_gates(page_table, cache_t, qm, n_full):
    db = page_table.shape[0]
    ppb = MOBA_BLOCK // PAGE_SIZE
    n_pages = n_full * ppb
    cpp = _tile(n_pages, 64)
    chunks_per_b = n_pages // cpp
    return pl.pallas_call(
        functools.partial(_page_gate_kernel, cpp=cpp, chunks_per_b=chunks_per_b),
        grid_spec=pltpu.PrefetchScalarGridSpec(
            num_scalar_prefetch=1,
            grid=(db * chunks_per_b,),
            in_specs=[pl.BlockSpec(memory_space=pl.ANY),
                      pl.BlockSpec((1, N_KV_HEADS, HEAD_DIM, N_HEADS),
                                   lambda s, pt: (s // chunks_per_b, 0, 0, 0))],
            out_specs=pl.BlockSpec((1, cpp // ppb, N_HEADS),
                                   lambda s, pt: (s // chunks_per_b, s % chunks_per_b, 0)),
            scratch_shapes=[pltpu.VMEM((2, cpp, N_KV_HEADS, HEAD_DIM, PAGE_SIZE), F32),
                            pltpu.SemaphoreType.DMA((2,))]),
        out_shape=jax.ShapeDtypeStruct((db, n_full, N_HEADS), F32),
        compiler_params=_params(dimension_semantics=("arbitrary",)),
        name="page_block_gates",
    )(page_table, cache_t, qm)


def _moba_select_kernel(gate_ref, sel_ref, *, bb, n_sel):
    for i in range(bb):
        gate = gate_ref[i]
        n_full = gate.shape[0]
        row = lax.broadcasted_iota(jnp.int32, gate.shape, 0).astype(F32)
        picks = []
        for _ in range(n_sel):
            best = jnp.max(gate, axis=0, keepdims=True)
            idx = jnp.min(jnp.where(gate == best, row, float(n_full)), axis=0, keepdims=True)
            picks.append(idx)
            gate = jnp.where(row == idx, -jnp.inf, gate)
        sel_ref[i] = jnp.concatenate(picks, axis=0).astype(jnp.int32)


def moba_select_sample(gates, n_sel):
    db, n_full, _ = gates.shape
    bb = _tile(db, 8)
    return pl.pallas_call(
        functools.partial(_moba_select_kernel, bb=bb, n_sel=n_sel),
        grid=(db // bb,),
        in_specs=[pl.BlockSpec((bb, n_full, N_HEADS), lambda i: (i, 0, 0))],
        out_specs=pl.BlockSpec((bb, n_sel, N_HEADS), lambda i: (i, 0, 0)),
        out_shape=jax.ShapeDtypeStruct((db, n_sel, N_HEADS), jnp.int32),
        compiler_params=_params(),
        name="moba_select_sample",
    )(gates)


def _moba_sample_kernel(phys_ref, tile_ref, ck_ref, cv_ref, q_ref, kn_ref, vn_ref, bt_ref, bias0_ref,
                        o_ref, kbuf_ref, vbuf_ref, s_ref, p_ref, sem_ref, *, n_sel):
    b = pl.program_id(0)
    n_b = pl.num_programs(0)
    ppb = MOBA_BLOCK // PAGE_SIZE
    n_pg = n_sel * ppb

    def copies(bi, slot):
        out = []
        for h in range(N_HEADS):
            g = h // GROUP
            for p in range(n_pg):
                page = phys_ref[bi, h * n_pg + p]
                out.append(pltpu.make_async_copy(ck_ref.at[page, g], kbuf_ref.at[slot, h, p],
                                                 sem_ref.at[0, slot]))
                out.append(pltpu.make_async_copy(cv_ref.at[page, g], vbuf_ref.at[slot, h, p],
                                                 sem_ref.at[1, slot]))
        return out

    @pl.when(b == 0)
    def _():
        for cp in copies(b, 0):
            cp.start()

    @pl.when(b + 1 < n_b)
    def _():
        for cp in copies(b + 1, (b + 1) % 2):
            cp.start()

    slot = b % 2
    for cp in copies(b, slot):
        cp.wait()

    rnd = lambda x: x.astype(BF16).astype(F32)
    n_rows = s_ref.shape[1]
    lane = lax.broadcasted_iota(jnp.int32, (1, PAGE_SIZE), 1)
    if n_rows > n_pg + 1:
        s_ref[:, n_pg + 1:, :] = jnp.full((N_HEADS, n_rows - n_pg - 1, PAGE_SIZE), NEG_INF, F32)
    for h in range(N_HEADS):
        g = h // GROUP
        q = rnd(q_ref[0, :, h:h + 1] * ATTN_SCALE)
        for si in range(n_sel):
            tile = tile_ref[b, h * n_sel + si]
            for pp in range(ppb):
                p = si * ppb + pp
                s_ref[h, p:p + 1, :] = (jnp.sum(rnd(kbuf_ref[slot, h, p]) * q, axis=0, keepdims=True)
                                        + bt_ref[tile, h, pp:pp + 1, :])
        s0 = (jnp.sum(q * rnd(kn_ref[0, :, g:g + 1]), axis=0, keepdims=True)
              + bias0_ref[:, h:h + 1])
        s_ref[h, n_pg:n_pg + 1, :] = jnp.where(lane == 0, s0, NEG_INF)
    s = s_ref[...]
    m = jnp.max(jnp.max(s, axis=2, keepdims=True), axis=1, keepdims=True)
    e = jnp.exp(s - m)
    den = jnp.sum(jnp.sum(e, axis=2, keepdims=True), axis=1, keepdims=True)
    p_ref[...] = rnd(e * (1.0 / den))
    head = lax.broadcasted_iota(jnp.int32, (1, N_HEADS), 1)
    o_all = jnp.zeros((HEAD_DIM, N_HEADS), F32)
    for h in range(N_HEADS):
        g = h // GROUP
        acc = rnd(vbuf_ref[slot, h, 0]) * p_ref[h, 0:1, :]
        for pg in range(1, n_pg):
            acc = acc + rnd(vbuf_ref[slot, h, pg]) * p_ref[h, pg:pg + 1, :]
        o = (jnp.sum(acc, axis=1, keepdims=True)
             + p_ref[h, n_pg:n_pg + 1, 0:1] * rnd(vn_ref[0, :, g:g + 1]))
        o_all = jnp.where(head == h, o, o_all)
    o_ref[0] = o_all


def moba_attn_sample(phys, tiles, cache_kt, cache_vt, q_cols, k_new, v_new, bias_tiles, bias0, n_sel):
    db = q_cols.shape[0]
    n_pg = n_sel * (MOBA_BLOCK // PAGE_SIZE)
    n_rows = -(-(n_pg + 1) // SUBLANES) * SUBLANES
    row = lambda b, *_: (b, 0, 0)
    return pl.pallas_call(
        functools.partial(_moba_sample_kernel, n_sel=n_sel),
        grid_spec=pltpu.PrefetchScalarGridSpec(
            num_scalar_prefetch=2,
            grid=(db,),
            in_specs=[pl.BlockSpec(memory_space=pl.ANY),
                      pl.BlockSpec(memory_space=pl.ANY),
                      pl.BlockSpec((1, HEAD_DIM, N_HEADS), row),
                      pl.BlockSpec((1, HEAD_DIM, N_KV_HEADS), row),
                      pl.BlockSpec((1, HEAD_DIM, N_KV_HEADS), row),
                      pl.BlockSpec(bias_tiles.shape, lambda b, *_: (0, 0, 0, 0)),
                      pl.BlockSpec((1, N_HEADS), lambda b, *_: (0, 0))],
            out_specs=pl.BlockSpec((1, HEAD_DIM, N_HEADS), row),
            scratch_shapes=[pltpu.VMEM((2, N_HEADS, n_pg, HEAD_DIM, PAGE_SIZE), F32),
                            pltpu.VMEM((2, N_HEADS, n_pg, HEAD_DIM, PAGE_SIZE), F32),
                            pltpu.VMEM((N_HEADS, n_rows, PAGE_SIZE), F32),
                            pltpu.VMEM((N_HEADS, n_rows, PAGE_SIZE), F32),
                            pltpu.SemaphoreType.DMA((2, 2))]),
        out_shape=jax.ShapeDtypeStruct((db, HEAD_DIM, N_HEADS), F32),
        compiler_params=_params(dimension_semantics=("arbitrary",)),
        name="moba_attn_sample",
    )(phys, tiles, cache_kt, cache_vt, q_cols, k_new, v_new, bias_tiles, bias0)


def _router_kernel(x_ref, g_ref, wr_ref, xn_ref, gate_ref, idx_ref):
    xn = _rms(x_ref[...], g_ref[...])
    xn_ref[...] = xn
    logits = jnp.dot(xn.astype(BF16), wr_ref[...], preferred_element_type=F32)
    n_e = logits.shape[1]
    col = lax.broadcasted_iota(jnp.int32, logits.shape, 1).astype(F32)
    m1 = jnp.max(logits, axis=1, keepdims=True)
    i1 = jnp.min(jnp.where(logits == m1, col, float(n_e)), axis=1, keepdims=True)
    rest = jnp.where(col == i1, -jnp.inf, logits)
    m2 = jnp.max(rest, axis=1, keepdims=True)
    i2 = jnp.min(jnp.where(rest == m2, col, float(n_e)), axis=1, keepdims=True)
    e2 = jnp.exp(m2 - m1)
    den = 1.0 + e2
    gate_ref[...] = jnp.concatenate([1.0 / den, e2 / den], axis=1)
    idx_ref[...] = jnp.concatenate([i1, i2], axis=1).astype(jnp.int32)


def moe_router(x, g, w_router, tm):
    m, d = x.shape
    n_e = w_router.shape[1]
    return pl.pallas_call(
        _router_kernel,
        grid=(m // tm,),
        in_specs=[pl.BlockSpec((tm, d), lambda i: (i, 0)),
                  pl.BlockSpec((1, d), lambda i: (0, 0)),
                  pl.BlockSpec((d, n_e), lambda i: (0, 0))],
        out_specs=[pl.BlockSpec((tm, d), lambda i: (i, 0)),
                   pl.BlockSpec((tm, EXPERT_TOPK), lambda i: (i, 0)),
                   pl.BlockSpec((tm, EXPERT_TOPK), lambda i: (i, 0))],
        out_shape=[jax.ShapeDtypeStruct((m, d), F32),
                   jax.ShapeDtypeStruct((m, EXPERT_TOPK), F32),
                   jax.ShapeDtypeStruct((m, EXPERT_TOPK), jnp.int32)],
        compiler_params=_params(),
        name="moe_router",
    )(x, g.reshape(1, d), w_router)


def _moe_ffn_kernel(te_ref, live_ref, x_ref, wg_ref, wu_ref, wd_ref, o_ref, xb_ref, acc_ref):
    i = pl.program_id(0)
    j = pl.program_id(1)

    @pl.when(j == 0)
    def _():
        xb_ref[...] = x_ref[...].astype(BF16)
        acc_ref[...] = jnp.zeros_like(acc_ref)

    @pl.when(live_ref[i] > 0)
    def _():
        x = xb_ref[...]
        a = jnp.dot(x, wg_ref[0].astype(BF16), preferred_element_type=F32)
        b = jnp.dot(x, wu_ref[0].astype(BF16), preferred_element_type=F32)
        h = (a * jax.nn.sigmoid(a)) * b
        acc_ref[...] += jnp.dot(h.astype(BF16), wd_ref[0].astype(BF16), preferred_element_type=F32)

    @pl.when(j == pl.num_programs(1) - 1)
    def _():
        o_ref[...] = acc_ref[...]


def moe_ffn(tile_expert, tile_live, x_sorted, wg, wu, wd, tm, tn):
    p, d = x_sorted.shape
    f = wg.shape[2]
    return pl.pallas_call(
        _moe_ffn_kernel,
        grid_spec=pltpu.PrefetchScalarGridSpec(
            num_scalar_prefetch=2,
            grid=(p // tm, f // tn),
            in_specs=[pl.BlockSpec((tm, d), lambda i, j, te, lv: (i, 0)),
                      pl.BlockSpec((1, d, tn), lambda i, j, te, lv: (te[i], 0, j)),
                      pl.BlockSpec((1, d, tn), lambda i, j, te, lv: (te[i], 0, j)),
                      pl.BlockSpec((1, tn, d), lambda i, j, te, lv: (te[i], j, 0))],
            out_specs=pl.BlockSpec((tm, d), lambda i, j, te, lv: (i, 0)),
            scratch_shapes=[pltpu.VMEM((tm, d), BF16), pltpu.VMEM((tm, d), F32)]),
        out_shape=jax.ShapeDtypeStruct((p, d), F32),
        compiler_params=_params(),
        name="moe_ffn",
    )(tile_expert, tile_live, x_sorted, wg, wu, wd)


def _combine_norm_kernel(y_ref, e1_ref, e2_ref, gate_ref, g_ref, o_ref):
    gate = gate_ref[...]
    y = y_ref[...] + (gate[:, 0:1] * e1_ref[...] + gate[:, 1:2] * e2_ref[...])
    o_ref[...] = _rms(y, g_ref[...])


def combine_norm(y, e1, e2, gates, g, tm):
    m, d = y.shape
    blk = pl.BlockSpec((tm, d), lambda i: (i, 0))
    return pl.pallas_call(
        _combine_norm_kernel,
        grid=(m // tm,),
        in_specs=[blk, blk, blk,
                  pl.BlockSpec((tm, EXPERT_TOPK), lambda i: (i, 0)),
                  pl.BlockSpec((1, d), lambda i: (0, 0))],
        out_specs=blk,
        out_shape=jax.ShapeDtypeStruct((m, d), F32),
        compiler_params=_params(),
        name="combine_norm",
    )(y, e1, e2, gates, g.reshape(1, d))


def moe_dispatch_plan(idx, tm):
    m, k = idx.shape
    n_asg = m * k
    p = -(-(n_asg + N_EXPERTS * (tm - 1)) // tm) * tm
    flat = idx.reshape(n_asg)
    onehot = (flat[:, None] == jnp.arange(N_EXPERTS, dtype=jnp.int32)[None, :]).astype(jnp.int32)
    csum = jnp.cumsum(onehot, axis=0)
    count = csum[-1]
    rank = jnp.take_along_axis(csum, flat[:, None], axis=1)[:, 0] - 1
    padded = -(-count // tm) * tm
    pad_end = jnp.cumsum(padded)
    pad_start = pad_end - padded
    slot = pad_start[flat] + rank
    order = jnp.sort(flat * n_asg + jnp.arange(n_asg, dtype=jnp.int32)) % n_asg
    start = jnp.cumsum(count) - count
    slots = jnp.arange(p, dtype=jnp.int32)
    slot_e = jnp.minimum(jnp.searchsorted(pad_end, slots, side="right"), N_EXPERTS - 1).astype(jnp.int32)
    r = slots - pad_start[slot_e]
    live = r < count[slot_e]
    src_asg = order[jnp.clip(start[slot_e] + r, 0, n_asg - 1)]
    row_src = jnp.where(live, src_asg // k, 0).astype(jnp.int32)
    tile_expert = slot_e[::tm]
    tile_live = live[::tm].astype(jnp.int32)
    return row_src, slot.reshape(m, k), tile_expert, tile_live


def _toeplitz_offsets(rows, cols):
    x = np.arange(rows + cols)
    return np.where(x < cols, x, x - (rows + cols))


def _toeplitz(src, rows, cols):
    period = rows + cols
    lead = src.shape[:-1]
    flat = jnp.tile(src, (1,) * len(lead) + (rows,))[..., :rows * (period - 1)]
    return flat.reshape(*lead, rows, period - 1)[..., :cols]


def _distance_bias(table, dist, valid):
    bias = jnp.where(valid[..., None], table[_bucket_np(dist)], NEG_INF)
    return jnp.swapaxes(bias, -1, -2)


def _window_prompt_bias(table):
    rows, cols = 2 * BAND_BLOCK, BAND_BLOCK
    dist = BAND_BLOCK + _toeplitz_offsets(rows, cols)
    bias = _toeplitz(_distance_bias(table, dist, (dist >= 0) & (dist <= WINDOW)), rows, cols)
    bias = bias.reshape(N_KV_HEADS, GROUP, rows, cols).transpose(0, 2, 1, 3)
    return bias.reshape(N_KV_HEADS, rows, QCOLS)


def _moba_prompt_bias(table):
    rows, cols = MOBA_BLOCK, MOBA_Q_BLOCK
    n_t = MOBA_FAR_TILE + 1
    u = np.arange(n_t)[:, None]
    dist = u * MOBA_Q_BLOCK + _toeplitz_offsets(rows, cols)[None, :]
    dist = np.where(u == MOBA_FAR_TILE, np.maximum(dist, FAR_DIST), dist)
    bias = _toeplitz(_distance_bias(table, dist, dist >= 0), rows, cols)
    bias = bias.reshape(n_t, N_KV_HEADS, GROUP, rows, cols).transpose(1, 0, 3, 2, 4)
    return bias.reshape(N_KV_HEADS, n_t, rows, QCOLS)


def _moba_sample_bias(table, past_len):
    ti = np.arange(SAMPLE_FAR_TILE + 1)[:, None]
    r = np.arange(MOBA_BLOCK)[None, :]
    dist = (ti + 1) * MOBA_BLOCK - r + (past_len % MOBA_BLOCK)
    dist = np.where(ti == SAMPLE_FAR_TILE, np.maximum(dist, FAR_DIST), dist)
    bias = table[_bucket_np(dist)].transpose(0, 2, 1)
    return bias.reshape(SAMPLE_FAR_TILE + 1, N_HEADS, MOBA_BLOCK // PAGE_SIZE, PAGE_SIZE)


def kernel(x_prompt, x_sample, state_win_k, state_win_v, cache_k, cache_v, page_table, rel_bias_table,
           attn_norm_g, ffn_norm_g, w_qkv_a, sink_a, w_o_a, kv_norm_g, w_kv_shared, w_q_b, w_o_b,
           w_gate_dense, w_up_dense, w_down_dense, w_router, w_gate_moe, w_up_moe, w_down_moe, final_norm_g):
    bp, sp, d = x_prompt.shape
    bs, ss, _ = x_sample.shape
    assert ss == 1 and attn_norm_g.shape[0] == 2
    mp = bp * sp
    hq = N_HEADS * HEAD_DIM
    hk = N_KV_HEADS * HEAD_DIM
    past_len = page_table.shape[1] * PAGE_SIZE
    assert past_len % MOBA_BLOCK == 0 and sp % MOBA_BLOCK == 0
    n_full = past_len // MOBA_BLOCK
    n_sel = min(MOBA_TOPK, n_full)
    w_buf = state_win_k.shape[2]
    table = rel_bias_table.astype(F32)

    tm_p = _tile(mp, 512)
    tm_s = _tile(bs, 512)
    tn = _tile(w_gate_dense.shape[2], 896) if w_gate_dense.shape[2] % 128 == 0 else w_gate_dense.shape[2]
    cast = lambda w: w.astype(BF16)

    yp = x_prompt.reshape(mp, d)
    ys = x_sample.reshape(bs, d)

    w_qkv = cast(w_qkv_a[0])
    qkv_p = rms_matmul(yp, attn_norm_g[0], w_qkv, tm_p)
    qkv_s = rms_matmul(ys, attn_norm_g[0], w_qkv, tm_s)

    def prompt_layouts(q, k, v):
        qt = q.reshape(bp, sp, N_HEADS, HEAD_DIM).transpose(0, 2, 3, 1)
        k4 = k.reshape(bp, sp, N_KV_HEADS, HEAD_DIM).transpose(0, 2, 1, 3)
        vt = v.reshape(bp, sp, N_KV_HEADS, HEAD_DIM).transpose(0, 2, 3, 1)
        return qt, k4, vt

    def from_heads_t(ot):
        return ot.transpose(0, 3, 1, 2).reshape(mp, hq)

    kp = qkv_p[:, hq:hq + hk]
    vp = qkv_p[:, hq + hk:]
    qt, k4, vt = prompt_layouts(qkv_p[:, :hq], kp, vp)
    sink_cols = jnp.repeat(sink_a[0].astype(F32).reshape(N_KV_HEADS, 1, GROUP), BAND_BLOCK, axis=2)
    mix_p = from_heads_t(window_attn_prompt(qt, k4, vt, _window_prompt_bias(table), sink_cols))

    head_mask = jnp.asarray((np.arange(N_HEADS)[:, None] // GROUP == np.arange(hk)[None, :] // HEAD_DIM)
                            .astype(np.float32))
    ks_new = qkv_s[:, hq:hq + hk]
    vs_new = qkv_s[:, hq + hk:]
    k_buf = state_win_k[:, 0].reshape(bs, w_buf, hk)
    v_buf = state_win_v[:, 0].reshape(bs, w_buf, hk)
    bias_ws = table[_bucket_np(w_buf - np.arange(w_buf))].T
    bias_0 = table[0][:, None]
    mix_s = window_attn_sample(qkv_s[:, :hq].reshape(bs, N_HEADS, HEAD_DIM), ks_new[:, None], vs_new[:, None],
                               k_buf, v_buf, bias_ws, bias_0, sink_a[0].astype(F32)[:, None], head_mask)

    w_o = cast(w_o_a[0])
    yp = matmul_res(mix_p, w_o, yp, tm_p)
    ys = matmul_res(mix_s.reshape(bs, hq), w_o, ys, tm_s)

    wg, wu, wd = cast(w_gate_dense[0]), cast(w_up_dense[0]), cast(w_down_dense[0])
    yp = ffn_dense(yp, ffn_norm_g[0], wg, wu, wd, tm_p, tn)
    ys = ffn_dense(ys, ffn_norm_g[0], wg, wu, wd, tm_s, tn)

    w_kv = cast(w_kv_shared)
    kv_p = rms_matmul(yp, kv_norm_g, w_kv, tm_p)
    kv_s = rms_matmul(ys, kv_norm_g, w_kv, tm_s)
    w_q = cast(w_q_b[0])
    q_p = rms_matmul(yp, attn_norm_g[1], w_q, tm_p)
    q_s = rms_matmul(ys, attn_norm_g[1], w_q, tm_s)

    qt, k4, vt = prompt_layouts(q_p, kv_p[:, :hk], kv_p[:, hk:])
    mix_p = from_heads_t(moba_attn_prompt(qt, k4, vt, _moba_prompt_bias(table)))

    cache_kt = cache_k.transpose(0, 2, 3, 1)
    cache_vt = cache_v.transpose(0, 2, 3, 1)
    q_cols = q_s.reshape(bs, N_HEADS, HEAD_DIM).transpose(0, 2, 1)
    group_mask = jnp.asarray((np.arange(N_KV_HEADS)[:, None, None] == np.arange(N_HEADS)[None, None, :] // GROUP)
                             .astype(np.float32))
    gates_s = page_block_gates(page_table, cache_kt, q_cols[:, None] * group_mask, n_full)
    sel = moba_select_sample(gates_s, n_sel).transpose(0, 2, 1)
    ppb = MOBA_BLOCK // PAGE_SIZE
    pages = sel[..., None] * ppb + jnp.arange(ppb, dtype=jnp.int32)
    phys = jnp.take_along_axis(page_table, pages.reshape(bs, -1), axis=1)
    tiles = jnp.clip(n_full - 1 - sel, 0, SAMPLE_FAR_TILE).reshape(bs, -1)
    to_cols = lambda t: t.reshape(bs, N_KV_HEADS, HEAD_DIM).transpose(0, 2, 1)
    o_cols = moba_attn_sample(phys, tiles, cache_kt, cache_vt, q_cols, to_cols(kv_s[:, :hk]),
                              to_cols(kv_s[:, hk:]), _moba_sample_bias(table, past_len), table[0][None, :],
                              n_sel)
    mix_s = o_cols.transpose(0, 2, 1)

    w_o = cast(w_o_b[0])
    yp = matmul_res(mix_p, w_o, yp, tm_p)
    ys = matmul_res(mix_s.reshape(bs, hq), w_o, ys, tm_s)

    y_all = jnp.concatenate([yp, ys], axis=0)
    m_all = mp + bs
    tm_a = _tile(m_all, 512)
    xn, gates, idx = moe_router(y_all, ffn_norm_g[1], cast(w_router[0]), tm_a)
    tm_e = 768
    f_moe = w_gate_moe.shape[3]
    tn_e = _tile(f_moe, 512) if f_moe % 128 == 0 else f_moe
    row_src, slot, tile_expert, tile_live = moe_dispatch_plan(idx, tm_e)
    x_sorted = jnp.take(xn, row_src, axis=0)
    e_sorted = moe_ffn(tile_expert, tile_live, x_sorted, w_gate_moe[0], w_up_moe[0], w_down_moe[0], tm_e, tn_e)
    e1 = jnp.take(e_sorted, slot[:, 0], axis=0)
    e2 = jnp.take(e_sorted, slot[:, 1], axis=0)
    y_out = combine_norm(y_all, e1, e2, gates, final_norm_g, tm_a)

    y_prompt = y_out[:mp].reshape(bp, sp, d)
    y_sample = y_out[mp:].reshape(bs, ss, d)
    w_keep = min(WINDOW, sp)
    kv4 = lambda t, b, s: t.reshape(b, s, N_KV_HEADS, HEAD_DIM)
    win_k_prompt = kv4(kp, bp, sp)[:, None, sp - w_keep:]
    win_v_prompt = kv4(vp, bp, sp)[:, None, sp - w_keep:]
    win_k_sample = jnp.concatenate([state_win_k[:, 0], kv4(ks_new, bs, 1)], axis=1)[:, None, ss:]
    win_v_sample = jnp.concatenate([state_win_v[:, 0], kv4(vs_new, bs, 1)], axis=1)[:, None, ss:]
    return (y_prompt, y_sample, win_k_prompt, win_v_prompt,
            kv4(kv_p[:, :hk], bp, sp), kv4(kv_p[:, hk:], bp, sp),
            win_k_sample, win_v_sample,
            kv4(kv_s[:, :hk], bs, 1), kv4(kv_s[:, hk:], bs, 1))
```

```python
import functools
import math

import numpy as np
import jax
import jax.numpy as jnp
from jax import lax
from jax.experimental import pallas as pl
from jax.experimental.pallas import tpu as pltpu

N_HEADS = 16
N_KV_HEADS = 4
HEAD_DIM = 64
GROUP = N_HEADS // N_KV_HEADS
WINDOW = 128
BAND_BLOCK = 128
MOBA_BLOCK = 256
MOBA_TOPK = 3
MOBA_Q_BLOCK = 128
PAGE_SIZE = 128
N_BUCKETS = 32
REL_MAX_DIST = 1024
N_EXPERTS = 8
EXPERT_TOPK = 2
RMS_EPS = 1e-5
NEG_INF = -1e30
ATTN_SCALE = HEAD_DIM ** -0.5
LOG2E = math.log2(math.e)
QCOLS = GROUP * MOBA_Q_BLOCK
FAR_DIST = 790
MOBA_FAR_TILE = 9
SAMPLE_FAR_TILE = 4
SUBLANES = 8

F32 = jnp.float32
BF16 = jnp.bfloat16
VMEM_LIMIT = 56 * 1024 * 1024


def _bucket_np(dist):
    max_exact = N_BUCKETS // 2
    n = np.maximum(np.asarray(dist, np.int64), 0)
    nf = np.maximum(n, 1).astype(np.float64)
    large = max_exact + (np.log(nf / max_exact) / math.log(REL_MAX_DIST / max_exact)
                         * (N_BUCKETS - max_exact)).astype(np.int64)
    return np.where(n < max_exact, n, np.minimum(large, N_BUCKETS - 1)).astype(np.int32)


def _tile(n, pref):
    if n <= pref:
        return n
    for t in range(pref, 7, -8):
        if n % t == 0:
            return t
    return n


def _params(**kw):
    return pltpu.CompilerParams(vmem_limit_bytes=VMEM_LIMIT, **kw)


def _rms(x, g):
    return (x * lax.rsqrt(jnp.mean(x * x, axis=-1, keepdims=True) + RMS_EPS)) * g


def _rms_matmul_kernel(x_ref, g_ref, w_ref, *o_refs):
    xn = _rms(x_ref[...], g_ref[...]).astype(BF16)
    z = jnp.dot(xn, w_ref[...], preferred_element_type=F32)
    col = 0
    for o_ref in o_refs:
        o_ref[...] = z[:, col:col + o_ref.shape[1]]
        col += o_ref.shape[1]


def rms_matmul(x, g, w, tm, widths=None):
    m, d = x.shape
    n = w.shape[1]
    widths = widths or (n,)
    assert sum(widths) == n
    outs = pl.pallas_call(
        _rms_matmul_kernel,
        grid=(m // tm,),
        in_specs=[pl.BlockSpec((tm, d), lambda i: (i, 0)),
                  pl.BlockSpec((1, d), lambda i: (0, 0)),
                  pl.BlockSpec((d, n), lambda i: (0, 0))],
        out_specs=[pl.BlockSpec((tm, wd), lambda i: (i, 0)) for wd in widths],
        out_shape=[jax.ShapeDtypeStruct((m, wd), F32) for wd in widths],
        compiler_params=_params(),
        name="rms_matmul",
    )(x, g.reshape(1, d), w)
    return outs if len(widths) > 1 else outs[0]


def _matmul_res_kernel(a_ref, w_ref, r_ref, o_ref):
    o_ref[...] = r_ref[...] + jnp.dot(a_ref[...].astype(BF16), w_ref[...], preferred_element_type=F32)


def matmul_res(a, w, res, tm):
    m, k = a.shape
    n = w.shape[1]
    return pl.pallas_call(
        _matmul_res_kernel,
        grid=(m // tm,),
        in_specs=[pl.BlockSpec((tm, k), lambda i: (i, 0)),
                  pl.BlockSpec((k, n), lambda i: (0, 0)),
                  pl.BlockSpec((tm, n), lambda i: (i, 0))],
        out_specs=pl.BlockSpec((tm, n), lambda i: (i, 0)),
        out_shape=jax.ShapeDtypeStruct((m, n), F32),
        compiler_params=_params(),
        name="matmul_res",
    )(a, w, res)


def _ffn_kernel(x_ref, g_ref, wg_ref, wu_ref, wd_ref, o_ref, xn_ref, acc_ref):
    j = pl.program_id(1)

    @pl.when(j == 0)
    def _():
        xn_ref[...] = _rms(x_ref[...], g_ref[...]).astype(BF16)
        acc_ref[...] = jnp.zeros_like(acc_ref)

    xn = xn_ref[...]
    a = jnp.dot(xn, wg_ref[...], preferred_element_type=F32)
    b = jnp.dot(xn, wu_ref[...], preferred_element_type=F32)
    h = (a * jax.nn.sigmoid(a)) * b
    acc_ref[...] += jnp.dot(h.astype(BF16), wd_ref[...], preferred_element_type=F32)

    @pl.when(j == pl.num_programs(1) - 1)
    def _():
        o_ref[...] = x_ref[...] + acc_ref[...]


def ffn_dense(x, g, wg, wu, wd, tm, tn):
    m, d = x.shape
    f = wg.shape[1]
    return pl.pallas_call(
        _ffn_kernel,
        grid=(m // tm, f // tn),
        in_specs=[pl.BlockSpec((tm, d), lambda i, j: (i, 0)),
                  pl.BlockSpec((1, d), lambda i, j: (0, 0)),
                  pl.BlockSpec((d, tn), lambda i, j: (0, j)),
                  pl.BlockSpec((d, tn), lambda i, j: (0, j)),
                  pl.BlockSpec((tn, d), lambda i, j: (j, 0))],
        out_specs=pl.BlockSpec((tm, d), lambda i, j: (i, 0)),
        out_shape=jax.ShapeDtypeStruct((m, d), F32),
        scratch_shapes=[pltpu.VMEM((tm, d), BF16), pltpu.VMEM((tm, d), F32)],
        compiler_params=_params(),
        name="ffn_dense",
    )(x, g.reshape(1, d), wg, wu, wd)


def _stack_heads(q_ref, g=0):
    return jnp.concatenate([q_ref[0, g * GROUP + r] for r in range(GROUP)], axis=1)


def _unstack_heads(o_ref, o, g=0):
    for r in range(GROUP):
        o_ref[0, g * GROUP + r] = o[:, r * MOBA_Q_BLOCK:(r + 1) * MOBA_Q_BLOCK].astype(o_ref.dtype)


def _win_prompt_kernel(q_ref, kp_ref, ko_ref, vp_ref, vo_ref, wb_ref, sink_ref, o_ref):
    n = pl.program_id(1)
    row = lax.broadcasted_iota(jnp.int32, (2 * BAND_BLOCK, QCOLS), 0)
    has_prev = (row >= BAND_BLOCK) | (n > 0)
    for g in range(N_KV_HEADS):
        q = (_stack_heads(q_ref, g) * ATTN_SCALE).astype(BF16)
        k = jnp.concatenate([kp_ref[0, g], ko_ref[0, g]], axis=0).astype(BF16)
        s = jnp.dot(k, q, preferred_element_type=F32) + wb_ref[g]
        s = jnp.where(has_prev, s, NEG_INF)
        sink = sink_ref[g]
        m = jnp.maximum(jnp.max(s, axis=0, keepdims=True), sink)
        e = jnp.exp(s - m)
        den = jnp.sum(e, axis=0, keepdims=True) + jnp.exp(sink - m)
        vt = jnp.concatenate([vp_ref[0, g], vo_ref[0, g]], axis=1).astype(BF16)
        o = jnp.dot(vt, e.astype(BF16), preferred_element_type=F32) / den
        _unstack_heads(o_ref, o, g)


def window_attn_prompt(qt, k4, vt, wb, sink_cols):
    b, _, _, s = qt.shape
    nb = s // BAND_BLOCK
    qb = BAND_BLOCK
    prev = lambda n: jnp.maximum(n - 1, 0)
    return pl.pallas_call(
        _win_prompt_kernel,
        grid=(b, nb),
        in_specs=[pl.BlockSpec((1, N_HEADS, HEAD_DIM, qb), lambda b, n: (b, 0, 0, n)),
                  pl.BlockSpec((1, N_KV_HEADS, qb, HEAD_DIM), lambda b, n: (b, 0, prev(n), 0)),
                  pl.BlockSpec((1, N_KV_HEADS, qb, HEAD_DIM), lambda b, n: (b, 0, n, 0)),
                  pl.BlockSpec((1, N_KV_HEADS, HEAD_DIM, qb), lambda b, n: (b, 0, 0, prev(n))),
                  pl.BlockSpec((1, N_KV_HEADS, HEAD_DIM, qb), lambda b, n: (b, 0, 0, n)),
                  pl.BlockSpec((N_KV_HEADS, 2 * qb, QCOLS), lambda b, n: (0, 0, 0)),
                  pl.BlockSpec((N_KV_HEADS, 1, QCOLS), lambda b, n: (0, 0, 0))],
        out_specs=pl.BlockSpec((1, N_HEADS, HEAD_DIM, qb), lambda b, n: (b, 0, 0, n)),
        out_shape=jax.ShapeDtypeStruct(qt.shape, BF16),
        compiler_params=_params(),
        name="window_attn_prompt",
    )(qt, k4, k4, vt, vt, wb, sink_cols)


def _moba_prompt_kernel(q_ref, k_ref, vt_ref, tb_ref, o_ref, kmean_ref, sel_ref, *, n_blk):
    qb = pl.program_id(2)
    tiles_per_blk = MOBA_BLOCK // MOBA_Q_BLOCK

    @pl.when(qb == 0)
    def _():
        kf = k_ref[0, 0].reshape(n_blk, MOBA_BLOCK, HEAD_DIM)
        kmean_ref[...] = jnp.mean(kf, axis=1)

    kmean = kmean_ref[...].astype(BF16)
    qs = []
    for tt in range(tiles_per_blk):
        cols = slice(tt * MOBA_Q_BLOCK, (tt + 1) * MOBA_Q_BLOCK)
        q32 = jnp.concatenate([q_ref[0, r][:, cols] for r in range(GROUP)], axis=1)
        gate = jnp.dot(kmean, q32.astype(BF16), preferred_element_type=F32)
        blk = lax.broadcasted_iota(jnp.int32, gate.shape, 0)
        past = blk < qb
        gate = jnp.where(past, gate, NEG_INF)
        rank = jnp.zeros(gate.shape, jnp.int32)
        for kk in range(n_blk):
            gk = gate[kk:kk + 1, :]
            beats = (gk > gate) | ((gk == gate) & (kk < blk))
            rank = rank + beats.astype(jnp.int32)
        sel = (past & (rank < MOBA_TOPK)) | (blk == qb)
        sel_ref[tt] = sel.astype(F32)
        qs.append((q32 * (ATTN_SCALE * LOG2E)).astype(BF16))

    def masked_logits(tt, s, j):
        u = jnp.clip(qb * tiles_per_blk + tt - tiles_per_blk * j, 0, MOBA_FAR_TILE)
        return jnp.where(sel_ref[tt, pl.ds(j, 1), :] > 0, s + tb_ref[0, u], NEG_INF)

    def body(jp, carry):
        off = pl.multiple_of(jp * (2 * MOBA_BLOCK), 2 * MOBA_BLOCK)
        kj = k_ref[0, 0, pl.ds(off, 2 * MOBA_BLOCK), :].astype(BF16)
        vj = vt_ref[0, 0, :, pl.ds(off, 2 * MOBA_BLOCK)].astype(BF16)
        out = []
        for tt in range(tiles_per_blk):
            m, l, acc = carry[3 * tt:3 * tt + 3]
            s = jnp.dot(kj, qs[tt], preferred_element_type=F32)
            s0 = masked_logits(tt, s[:MOBA_BLOCK], 2 * jp)
            s1 = masked_logits(tt, s[MOBA_BLOCK:], 2 * jp + 1)
            m_new = jnp.maximum(m, jnp.maximum(jnp.max(s0, axis=0, keepdims=True),
                                               jnp.max(s1, axis=0, keepdims=True)))
            alpha = jnp.exp2(m - m_new)
            p0 = jnp.exp2(s0 - m_new)
            p1 = jnp.exp2(s1 - m_new)
            l = alpha * l + (jnp.sum(p0, axis=0, keepdims=True) + jnp.sum(p1, axis=0, keepdims=True))
            acc = (alpha * acc
                   + jnp.dot(vj[:, :MOBA_BLOCK], p0.astype(BF16), preferred_element_type=F32)
                   + jnp.dot(vj[:, MOBA_BLOCK:], p1.astype(BF16), preferred_element_type=F32))
            out += [m_new, l, acc]
        return tuple(out)

    init = (jnp.full((1, QCOLS), NEG_INF, F32), jnp.zeros((1, QCOLS), F32),
            jnp.zeros((HEAD_DIM, QCOLS), F32)) * tiles_per_blk
    final = lax.fori_loop(0, qb // 2 + 1, body, init)
    for tt in range(tiles_per_blk):
        m, l, acc = final[3 * tt:3 * tt + 3]
        o = acc / l
        for r in range(GROUP):
            o_ref[0, r, :, tt * MOBA_Q_BLOCK:(tt + 1) * MOBA_Q_BLOCK] = (
                o[:, r * MOBA_Q_BLOCK:(r + 1) * MOBA_Q_BLOCK].astype(o_ref.dtype))


def moba_attn_prompt(qt, k4, vt, tb):
    b, _, _, s = qt.shape
    n_blk = s // MOBA_BLOCK
    assert n_blk % 2 == 0, "kv blocks are swept in pairs"
    n_tb = tb.shape[1]
    return pl.pallas_call(
        functools.partial(_moba_prompt_kernel, n_blk=n_blk),
        grid=(N_KV_HEADS, b, n_blk),
        in_specs=[pl.BlockSpec((1, GROUP, HEAD_DIM, MOBA_BLOCK), lambda g, b, t: (b, g, 0, t)),
                  pl.BlockSpec((1, 1, s, HEAD_DIM), lambda g, b, t: (b, g, 0, 0)),
                  pl.BlockSpec((1, 1, HEAD_DIM, s), lambda g, b, t: (b, g, 0, 0)),
                  pl.BlockSpec((1, n_tb, MOBA_BLOCK, QCOLS), lambda g, b, t: (g, 0, 0, 0))],
        out_specs=pl.BlockSpec((1, GROUP, HEAD_DIM, MOBA_BLOCK), lambda g, b, t: (b, g, 0, t)),
        out_shape=jax.ShapeDtypeStruct(qt.shape, BF16),
        scratch_shapes=[pltpu.VMEM((n_blk, HEAD_DIM), F32),
                        pltpu.VMEM((MOBA_BLOCK // MOBA_Q_BLOCK, n_blk, QCOLS), F32)],
        compiler_params=_params(),
        name="moba_attn_prompt",
    )(qt, k4, vt, tb)


def _expand_heads(q, mask):
    return jnp.concatenate([q] * N_KV_HEADS, axis=1) * mask


def _fold_heads(o):
    acc = o
    for c in range(1, N_KV_HEADS):
        acc = acc + pltpu.roll(o, c * HEAD_DIM, axis=1)
    return acc[:, :HEAD_DIM]


def _win_sample_kernel(q_ref, kn_ref, vn_ref, kb_ref, vb_ref, bias_ref, bias0_ref, sink_ref,
                       mask_ref, o_ref, *, bb):
    mask = mask_ref[...]
    sink = sink_ref[...]
    rnd = lambda x: x.astype(BF16).astype(F32)
    for i in range(bb):
        qe = _expand_heads(q_ref[i] * ATTN_SCALE, mask).astype(BF16)
        s = lax.dot_general(qe, kb_ref[i].astype(BF16), (((1,), (1,)), ((), ())),
                            preferred_element_type=F32) + bias_ref[...]
        s0 = jnp.sum(qe.astype(F32) * rnd(kn_ref[i]), axis=1, keepdims=True) + bias0_ref[...]
        m = jnp.maximum(jnp.maximum(jnp.max(s, axis=1, keepdims=True), s0), sink)
        e = jnp.exp(s - m)
        e0 = jnp.exp(s0 - m)
        inv = 1.0 / (jnp.sum(e, axis=1, keepdims=True) + e0 + jnp.exp(sink - m))
        o = (jnp.dot((e * inv).astype(BF16), vb_ref[i].astype(BF16), preferred_element_type=F32)
             + rnd(e0 * inv) * rnd(vn_ref[i]))
        o_ref[i] = _fold_heads(o * mask)


def window_attn_sample(q3, k_new, v_new, k_buf, v_buf, bias, bias0, sink, mask):
    db, w, kvw = k_buf.shape
    bb = _tile(db, 8)
    row = lambda i: (i, 0, 0)
    fixed = lambda i: (0, 0)
    return pl.pallas_call(
        functools.partial(_win_sample_kernel, bb=bb),
        grid=(db // bb,),
        in_specs=[pl.BlockSpec((bb, N_HEADS, HEAD_DIM), row),
                  pl.BlockSpec((bb, 1, kvw), row),
                  pl.BlockSpec((bb, 1, kvw), row),
                  pl.BlockSpec((bb, w, kvw), row),
                  pl.BlockSpec((bb, w, kvw), row),
                  pl.BlockSpec((N_HEADS, w), fixed),
                  pl.BlockSpec((N_HEADS, 1), fixed),
                  pl.BlockSpec((N_HEADS, 1), fixed),
                  pl.BlockSpec((N_HEADS, kvw), fixed)],
        out_specs=pl.BlockSpec((bb, N_HEADS, HEAD_DIM), row),
        out_shape=jax.ShapeDtypeStruct((db, N_HEADS, HEAD_DIM), F32),
        compiler_params=_params(),
        name="window_attn_sample",
    )(q3, k_new, v_new, k_buf, v_buf, bias, bias0, sink, mask)


def _copy_page(cache_ref, page, dst_ref, sem):
    return pltpu.make_async_copy(cache_ref.at[page], dst_ref, sem)


def _page_gate_kernel(pt_ref, cache_ref, qm_ref, o_ref, buf_ref, sem_ref, *, cpp, chunks_per_b):
    step = pl.program_id(0)
    n_steps = pl.num_programs(0)

    def copies(s, slot):
        b = s // chunks_per_b
        c = s % chunks_per_b
        return [_copy_page(cache_ref, pt_ref[b, c * cpp + p], buf_ref.at[slot, p], sem_ref.at[slot])
                for p in range(cpp)]

    @pl.when(step == 0)
    def _():
        for cp in copies(step, 0):
            cp.start()

    @pl.when(step + 1 < n_steps)
    def _():
        for cp in copies(step + 1, (step + 1) % 2):
            cp.start()

    slot = step % 2
    for cp in copies(step, slot):
        cp.wait()
    ppb = MOBA_BLOCK // PAGE_SIZE
    qm = qm_ref[0].astype(BF16).astype(F32)
    for n in range(cpp // ppb):
        x = jnp.sum(buf_ref[slot, pl.ds(n * ppb, ppb)], axis=0)
        mean = jnp.sum(x, axis=-1, keepdims=True) * (1.0 / MOBA_BLOCK)
        mean = mean.astype(BF16).astype(F32)
        o_ref[0, n:n + 1, :] = jnp.sum(jnp.sum(mean * qm, axis=0), axis=0, keepdims=True)


def page_block_gates(page_table, cache_t, qm, n_full):
    db = page_table.shape[0]
    ppb = MOBA_BLOCK // PAGE_SIZE
    n_pages = n_full * ppb
    cpp = _tile(n_pages, 64)
    chunks_per_b = n_pages // cpp
    return pl.pallas_call(
        functools.partial(_page_gate_kernel, cpp=cpp, chunks_per_b=chunks_per_b),
        grid_spec=pltpu.PrefetchScalarGridSpec(
            num_scalar_prefetch=1,
            grid=(db * chunks_per_b,),
            in_specs=[pl.BlockSpec(memory_space=pl.ANY),
                      pl.BlockSpec((1, N_KV_HEADS, HEAD_DIM, N_HEADS),
                                   lambda s, pt: (s // chunks_per_b, 0, 0, 0))],
            out_specs=pl.BlockSpec((1, cpp // ppb, N_HEADS),
                                   lambda s, pt: (s // chunks_per_b, s % chunks_per_b, 0)),
            scratch_shapes=[pltpu.VMEM((2, cpp, N_KV_HEADS, HEAD_DIM, PAGE_SIZE), F32),
                            pltpu.SemaphoreType.DMA((2,))]),
        out_shape=jax.ShapeDtypeStruct((db, n_full, N_HEADS), F32),
        compiler_params=_params(dimension_semantics=("arbitrary",)),
        name="page_block_gates",
    )(page_table, cache_t, qm)


def _moba_select_kernel(gate_ref, sel_ref, *, bb, n_sel):
    for i in range(bb):
        gate = gate_ref[i]
        n_full = gate.shape[0]
        row = lax.broadcasted_iota(jnp.int32, gate.shape, 0).astype(F32)
        picks = []
        for _ in range(n_sel):
            best = jnp.max(gate, axis=0, keepdims=True)
            idx = jnp.min(jnp.where(gate == best, row, float(n_full)), axis=0, keepdims=True)
            picks.append(idx)
            gate = jnp.where(row == idx, -jnp.inf, gate)
        sel_ref[i] = jnp.concatenate(picks, axis=0).astype(jnp.int32)


def moba_select_sample(gates, n_sel):
    db, n_full, _ = gates.shape
    bb = _tile(db, 8)
    return pl.pallas_call(
        functools.partial(_moba_select_kernel, bb=bb, n_sel=n_sel),
        grid=(db // bb,),
        in_specs=[pl.BlockSpec((bb, n_full, N_HEADS), lambda i: (i, 0, 0))],
        out_specs=pl.BlockSpec((bb, n_sel, N_HEADS), lambda i: (i, 0, 0)),
        out_shape=jax.ShapeDtypeStruct((db, n_sel, N_HEADS), jnp.int32),
        compiler_params=_params(),
        name="moba_select_sample",
    )(gates)


def _moba_sample_kernel(phys_ref, tile_ref, ck_ref, cv_ref, q_ref, kn_ref, vn_ref, bt_ref, bias0_ref,
                        o_ref, kbuf_ref, vbuf_ref, s_ref, p_ref, sem_ref, *, n_sel):
    b = pl.program_id(0)
    n_b = pl.num_programs(0)
    ppb = MOBA_BLOCK // PAGE_SIZE
    n_pg = n_sel * ppb

    def copies(bi, slot):
        out = []
        for h in range(N_HEADS):
            g = h // GROUP
            for p in range(n_pg):
                page = phys_ref[bi, h * n_pg + p]
                out.append(pltpu.make_async_copy(ck_ref.at[page, g], kbuf_ref.at[slot, h, p],
                                                 sem_ref.at[0, slot]))
                out.append(pltpu.make_async_copy(cv_ref.at[page, g], vbuf_ref.at[slot, h, p],
                                                 sem_ref.at[1, slot]))
        return out

    @pl.when(b == 0)
    def _():
        for cp in copies(b, 0):
            cp.start()

    @pl.when(b + 1 < n_b)
    def _():
        for cp in copies(b + 1, (b + 1) % 2):
            cp.start()

    slot = b % 2
    for cp in copies(b, slot):
        cp.wait()

    rnd = lambda x: x.astype(BF16).astype(F32)
    n_rows = s_ref.shape[1]
    lane = lax.broadcasted_iota(jnp.int32, (1, PAGE_SIZE), 1)
    if n_rows > n_pg + 1:
        s_ref[:, n_pg + 1:, :] = jnp.full((N_HEADS, n_rows - n_pg - 1, PAGE_SIZE), NEG_INF, F32)
    for h in range(N_HEADS):
        g = h // GROUP
        q = rnd(q_ref[0, :, h:h + 1] * ATTN_SCALE)
        for si in range(n_sel):
            tile = tile_ref[b, h * n_sel + si]
            for pp in range(ppb):
                p = si * ppb + pp
                s_ref[h, p:p + 1, :] = (jnp.sum(rnd(kbuf_ref[slot, h, p]) * q, axis=0, keepdims=True)
                                        + bt_ref[tile, h, pp:pp + 1, :])
        s0 = (jnp.sum(q * rnd(kn_ref[0, :, g:g + 1]), axis=0, keepdims=True)
              + bias0_ref[:, h:h + 1])
        s_ref[h, n_pg:n_pg + 1, :] = jnp.where(lane == 0, s0, NEG_INF)
    s = s_ref[...]
    m = jnp.max(jnp.max(s, axis=2, keepdims=True), axis=1, keepdims=True)
    e = jnp.exp(s - m)
    den = jnp.sum(jnp.sum(e, axis=2, keepdims=True), axis=1, keepdims=True)
    p_ref[...] = rnd(e * (1.0 / den))
    head = lax.broadcasted_iota(jnp.int32, (1, N_HEADS), 1)
    o_all = jnp.zeros((HEAD_DIM, N_HEADS), F32)
    for h in range(N_HEADS):
        g = h // GROUP
        acc = rnd(vbuf_ref[slot, h, 0]) * p_ref[h, 0:1, :]
        for pg in range(1, n_pg):
            acc = acc + rnd(vbuf_ref[slot, h, pg]) * p_ref[h, pg:pg + 1, :]
        o = (jnp.sum(acc, axis=1, keepdims=True)
             + p_ref[h, n_pg:n_pg + 1, 0:1] * rnd(vn_ref[0, :, g:g + 1]))
        o_all = jnp.where(head == h, o, o_all)
    o_ref[0] = o_all


def moba_attn_sample(phys, tiles, cache_kt, cache_vt, q_cols, k_new, v_new, bias_tiles, bias0, n_sel):
    db = q_cols.shape[0]
    n_pg = n_sel * (MOBA_BLOCK // PAGE_SIZE)
    n_rows = -(-(n_pg + 1) // SUBLANES) * SUBLANES
    row = lambda b, *_: (b, 0, 0)
    return pl.pallas_call(
        functools.partial(_moba_sample_kernel, n_sel=n_sel),
        grid_spec=pltpu.PrefetchScalarGridSpec(
            num_scalar_prefetch=2,
            grid=(db,),
            in_specs=[pl.BlockSpec(memory_space=pl.ANY),
                      pl.BlockSpec(memory_space=pl.ANY),
                      pl.BlockSpec((1, HEAD_DIM, N_HEADS), row),
                      pl.BlockSpec((1, HEAD_DIM, N_KV_HEADS), row),
                      pl.BlockSpec((1, HEAD_DIM, N_KV_HEADS), row),
                      pl.BlockSpec(bias_tiles.shape, lambda b, *_: (0, 0, 0, 0)),
                      pl.BlockSpec((1, N_HEADS), lambda b, *_: (0, 0))],
            out_specs=pl.BlockSpec((1, HEAD_DIM, N_HEADS), row),
            scratch_shapes=[pltpu.VMEM((2, N_HEADS, n_pg, HEAD_DIM, PAGE_SIZE), F32),
                            pltpu.VMEM((2, N_HEADS, n_pg, HEAD_DIM, PAGE_SIZE), F32),
                            pltpu.VMEM((N_HEADS, n_rows, PAGE_SIZE), F32),
                            pltpu.VMEM((N_HEADS, n_rows, PAGE_SIZE), F32),
                            pltpu.SemaphoreType.DMA((2, 2))]),
        out_shape=jax.ShapeDtypeStruct((db, HEAD_DIM, N_HEADS), F32),
        compiler_params=_params(dimension_semantics=("arbitrary",)),
        name="moba_attn_sample",
    )(phys, tiles, cache_kt, cache_vt, q_cols, k_new, v_new, bias_tiles, bias0)


def _router_kernel(x_ref, g_ref, wr_ref, xn_ref, gate_ref, idx_ref):
    xn = _rms(x_ref[...], g_ref[...])
    xn_ref[...] = xn
    logits = jnp.dot(xn.astype(BF16), wr_ref[...], preferred_element_type=F32)
    n_e = logits.shape[1]
    col = lax.broadcasted_iota(jnp.int32, logits.shape, 1).astype(F32)
    m1 = jnp.max(logits, axis=1, keepdims=True)
    i1 = jnp.min(jnp.where(logits == m1, col, float(n_e)), axis=1, keepdims=True)
    rest = jnp.where(col == i1, -jnp.inf, logits)
    m2 = jnp.max(rest, axis=1, keepdims=True)
    i2 = jnp.min(jnp.where(rest == m2, col, float(n_e)), axis=1, keepdims=True)
    e2 = jnp.exp(m2 - m1)
    den = 1.0 + e2
    gate_ref[...] = jnp.concatenate([1.0 / den, e2 / den], axis=1)
    idx_ref[...] = jnp.concatenate([i1, i2], axis=1).astype(jnp.int32)


def _router_into_kernel(x_ref, g_ref, wr_ref, prev_ref, xn_ref, gate_ref, idx_ref):
    del prev_ref
    _router_kernel(x_ref, g_ref, wr_ref, xn_ref, gate_ref, idx_ref)


def moe_router(x, g, w_router, tm, xn_all, row0):
    m, d = x.shape
    n_e = w_router.shape[1]
    assert row0 % tm == 0
    first = row0 // tm
    return pl.pallas_call(
        _router_into_kernel,
        grid=(m // tm,),
        in_specs=[pl.BlockSpec((tm, d), lambda i: (i, 0)),
                  pl.BlockSpec((1, d), lambda i: (0, 0)),
                  pl.BlockSpec((d, n_e), lambda i: (0, 0)),
                  pl.BlockSpec(memory_space=pl.ANY)],
        out_specs=[pl.BlockSpec((tm, d), lambda i: (first + i, 0)),
                   pl.BlockSpec((tm, EXPERT_TOPK), lambda i: (i, 0)),
                   pl.BlockSpec((tm, EXPERT_TOPK), lambda i: (i, 0))],
        out_shape=[jax.ShapeDtypeStruct(xn_all.shape, F32),
                   jax.ShapeDtypeStruct((m, EXPERT_TOPK), F32),
                   jax.ShapeDtypeStruct((m, EXPERT_TOPK), jnp.int32)],
        input_output_aliases={3: 0},
        compiler_params=_params(),
        name="moe_router",
    )(x, g.reshape(1, d), w_router, xn_all)


def _moe_ffn_kernel(te_ref, live_ref, x_ref, wg_ref, wu_ref, wd_ref, prev_ref, o_ref, xb_ref, acc_ref):
    del prev_ref
    i = pl.program_id(0)
    j = pl.program_id(1)

    @pl.when(j == 0)
    def _():
        xb_ref[...] = x_ref[...].astype(BF16)
        acc_ref[...] = jnp.zeros_like(acc_ref)

    @pl.when(live_ref[i] > 0)
    def _():
        x = xb_ref[...]
        a = jnp.dot(x, wg_ref[0].astype(BF16), preferred_element_type=F32)
        b = jnp.dot(x, wu_ref[0].astype(BF16), preferred_element_type=F32)
        h = (a * jax.nn.sigmoid(a)) * b
        acc_ref[...] += jnp.dot(h.astype(BF16), wd_ref[0].astype(BF16), preferred_element_type=F32)

    @pl.when(j == pl.num_programs(1) - 1)
    def _():
        o_ref[...] = acc_ref[...]


def moe_ffn(tile_expert, tile_live, x_sorted, wg, wu, wd, e_sorted, first_tile, tm, tn):
    n_tiles = tile_expert.shape[0]
    d = x_sorted.shape[1]
    f = wg.shape[2]
    return pl.pallas_call(
        _moe_ffn_kernel,
        grid_spec=pltpu.PrefetchScalarGridSpec(
            num_scalar_prefetch=2,
            grid=(n_tiles, f // tn),
            in_specs=[pl.BlockSpec((tm, d), lambda i, j, te, lv: (i, 0)),
                      pl.BlockSpec((1, d, tn), lambda i, j, te, lv: (te[i], 0, j)),
                      pl.BlockSpec((1, d, tn), lambda i, j, te, lv: (te[i], 0, j)),
                      pl.BlockSpec((1, tn, d), lambda i, j, te, lv: (te[i], j, 0)),
                      pl.BlockSpec(memory_space=pl.ANY)],
            out_specs=pl.BlockSpec((tm, d), lambda i, j, te, lv: (first_tile + i, 0)),
            scratch_shapes=[pltpu.VMEM((tm, d), BF16), pltpu.VMEM((tm, d), F32)]),
        out_shape=jax.ShapeDtypeStruct(e_sorted.shape, F32),
        input_output_aliases={6: 0},
        compiler_params=_params(),
        name="moe_ffn",
    )(tile_expert, tile_live, x_sorted, wg, wu, wd, e_sorted)


def _combine_norm_kernel(y_ref, e1_ref, e2_ref, gate_ref, g_ref, o_ref):
    gate = gate_ref[...]
    y = y_ref[...] + (gate[:, 0:1] * e1_ref[...] + gate[:, 1:2] * e2_ref[...])
    o_ref[...] = _rms(y, g_ref[...])


def combine_norm(y, e1, e2, gates, g, tm):
    m, d = y.shape
    blk = pl.BlockSpec((tm, d), lambda i: (i, 0))
    return pl.pallas_call(
        _combine_norm_kernel,
        grid=(m // tm,),
        in_specs=[blk, blk, blk,
                  pl.BlockSpec((tm, EXPERT_TOPK), lambda i: (i, 0)),
                  pl.BlockSpec((1, d), lambda i: (0, 0))],
        out_specs=blk,
        out_shape=jax.ShapeDtypeStruct((m, d), F32),
        compiler_params=_params(),
        name="combine_norm",
    )(y, e1, e2, gates, g.reshape(1, d))


def moe_dispatch_plan(idx, tm, n_chunks):
    m, k = idx.shape
    n_asg = m * k
    p = -(-(n_asg + N_EXPERTS * (tm - 1)) // (tm * n_chunks)) * (tm * n_chunks)
    flat = idx.reshape(n_asg)
    onehot = (flat[:, None] == jnp.arange(N_EXPERTS, dtype=jnp.int32)[None, :]).astype(jnp.int32)
    csum = jnp.cumsum(onehot, axis=0)
    count = csum[-1]
    rank = jnp.take_along_axis(csum, flat[:, None], axis=1)[:, 0] - 1
    padded = -(-count // tm) * tm
    pad_end = jnp.cumsum(padded)
    pad_start = pad_end - padded
    slot = pad_start[flat] + rank
    order = jnp.sort(flat * n_asg + jnp.arange(n_asg, dtype=jnp.int32)) % n_asg
    start = jnp.cumsum(count) - count
    slots = jnp.arange(p, dtype=jnp.int32)
    slot_e = jnp.minimum(jnp.searchsorted(pad_end, slots, side="right"), N_EXPERTS - 1).astype(jnp.int32)
    r = slots - pad_start[slot_e]
    live = r < count[slot_e]
    src_asg = order[jnp.clip(start[slot_e] + r, 0, n_asg - 1)]
    row_src = jnp.where(live, src_asg // k, 0).astype(jnp.int32)
    tile_expert = slot_e[::tm]
    tile_live = live[::tm].astype(jnp.int32)
    return row_src, slot.reshape(m, k), tile_expert, tile_live


def _toeplitz_offsets(rows, cols):
    x = np.arange(rows + cols)
    return np.where(x < cols, x, x - (rows + cols))


def _toeplitz(src, rows, cols):
    period = rows + cols
    lead = src.shape[:-1]
    flat = jnp.tile(src, (1,) * len(lead) + (rows,))[..., :rows * (period - 1)]
    return flat.reshape(*lead, rows, period - 1)[..., :cols]


def _distance_bias(table, dist, valid):
    bias = jnp.where(valid[..., None], table[_bucket_np(dist)], NEG_INF)
    return jnp.swapaxes(bias, -1, -2)


def _window_prompt_bias(table):
    rows, cols = 2 * BAND_BLOCK, BAND_BLOCK
    dist = BAND_BLOCK + _toeplitz_offsets(rows, cols)
    bias = _toeplitz(_distance_bias(table, dist, (dist >= 0) & (dist <= WINDOW)), rows, cols)
    bias = bias.reshape(N_KV_HEADS, GROUP, rows, cols).transpose(0, 2, 1, 3)
    return bias.reshape(N_KV_HEADS, rows, QCOLS)


def _moba_prompt_bias(table):
    rows, cols = MOBA_BLOCK, MOBA_Q_BLOCK
    n_t = MOBA_FAR_TILE + 1
    u = np.arange(n_t)[:, None]
    dist = u * MOBA_Q_BLOCK + _toeplitz_offsets(rows, cols)[None, :]
    dist = np.where(u == MOBA_FAR_TILE, np.maximum(dist, FAR_DIST), dist)
    bias = _toeplitz(_distance_bias(table, dist, dist >= 0), rows, cols)
    bias = bias.reshape(n_t, N_KV_HEADS, GROUP, rows, cols).transpose(1, 0, 3, 2, 4)
    return bias.reshape(N_KV_HEADS, n_t, rows, QCOLS)


def _moba_sample_bias(table, past_len):
    ti = np.arange(SAMPLE_FAR_TILE + 1)[:, None]
    r = np.arange(MOBA_BLOCK)[None, :]
    dist = (ti + 1) * MOBA_BLOCK - r + (past_len % MOBA_BLOCK)
    dist = np.where(ti == SAMPLE_FAR_TILE, np.maximum(dist, FAR_DIST), dist)
    bias = table[_bucket_np(dist)].transpose(0, 2, 1)
    return bias.reshape(SAMPLE_FAR_TILE + 1, N_HEADS, MOBA_BLOCK // PAGE_SIZE, PAGE_SIZE)


def kernel(x_prompt, x_sample, state_win_k, state_win_v, cache_k, cache_v, page_table, rel_bias_table,
           attn_norm_g, ffn_norm_g, w_qkv_a, sink_a, w_o_a, kv_norm_g, w_kv_shared, w_q_b, w_o_b,
           w_gate_dense, w_up_dense, w_down_dense, w_router, w_gate_moe, w_up_moe, w_down_moe, final_norm_g):
    bp, sp, d = x_prompt.shape
    bs, ss, _ = x_sample.shape
    assert ss == 1 and attn_norm_g.shape[0] == 2
    mp = bp * sp
    hq = N_HEADS * HEAD_DIM
    hk = N_KV_HEADS * HEAD_DIM
    past_len = page_table.shape[1] * PAGE_SIZE
    assert past_len % MOBA_BLOCK == 0 and sp % MOBA_BLOCK == 0
    n_full = past_len // MOBA_BLOCK
    n_sel = min(MOBA_TOPK, n_full)
    w_buf = state_win_k.shape[2]
    table = rel_bias_table.astype(F32)

    tm_p = _tile(mp, 512)
    tm_s = _tile(bs, 512)
    tn = _tile(w_gate_dense.shape[2], 896) if w_gate_dense.shape[2] % 128 == 0 else w_gate_dense.shape[2]
    cast = lambda w: w.astype(BF16)

    yp = x_prompt.reshape(mp, d)
    ys = x_sample.reshape(bs, d)

    w_qkv = cast(w_qkv_a[0])
    qp, kp, vp = rms_matmul(yp, attn_norm_g[0], w_qkv, tm_p, (hq, hk, hk))
    qs_new, ks_new, vs_new = rms_matmul(ys, attn_norm_g[0], w_qkv, tm_s, (hq, hk, hk))

    def prompt_layouts(q, k, v):
        qt = q.reshape(bp, sp, N_HEADS, HEAD_DIM).transpose(0, 2, 3, 1)
        k4 = k.reshape(bp, sp, N_KV_HEADS, HEAD_DIM).transpose(0, 2, 1, 3)
        vt = v.reshape(bp, sp, N_KV_HEADS, HEAD_DIM).transpose(0, 2, 3, 1)
        return qt, k4, vt

    def from_heads_t(ot):
        return ot.transpose(0, 3, 1, 2).reshape(mp, hq)

    qt, k4, vt = prompt_layouts(qp, kp, vp)
    sink_cols = jnp.repeat(sink_a[0].astype(F32).reshape(N_KV_HEADS, 1, GROUP), BAND_BLOCK, axis=2)
    mix_p = from_heads_t(window_attn_prompt(qt, k4, vt, _window_prompt_bias(table), sink_cols))

    head_mask = jnp.asarray((np.arange(N_HEADS)[:, None] // GROUP == np.arange(hk)[None, :] // HEAD_DIM)
                            .astype(np.float32))
    k_buf = state_win_k[:, 0].reshape(bs, w_buf, hk)
    v_buf = state_win_v[:, 0].reshape(bs, w_buf, hk)
    bias_ws = table[_bucket_np(w_buf - np.arange(w_buf))].T
    bias_0 = table[0][:, None]
    mix_s = window_attn_sample(qs_new.reshape(bs, N_HEADS, HEAD_DIM), ks_new[:, None], vs_new[:, None],
                               k_buf, v_buf, bias_ws, bias_0, sink_a[0].astype(F32)[:, None], head_mask)

    w_o = cast(w_o_a[0])
    yp = matmul_res(mix_p, w_o, yp, tm_p)
    ys = matmul_res(mix_s.reshape(bs, hq), w_o, ys, tm_s)

    wg, wu, wd = cast(w_gate_dense[0]), cast(w_up_dense[0]), cast(w_down_dense[0])
    yp = ffn_dense(yp, ffn_norm_g[0], wg, wu, wd, tm_p, tn)
    ys = ffn_dense(ys, ffn_norm_g[0], wg, wu, wd, tm_s, tn)

    w_kv = cast(w_kv_shared)
    kv_kp, kv_vp = rms_matmul(yp, kv_norm_g, w_kv, tm_p, (hk, hk))
    kv_ks, kv_vs = rms_matmul(ys, kv_norm_g, w_kv, tm_s, (hk, hk))
    w_q = cast(w_q_b[0])
    q_p = rms_matmul(yp, attn_norm_g[1], w_q, tm_p)
    q_s = rms_matmul(ys, attn_norm_g[1], w_q, tm_s)

    qt, k4, vt = prompt_layouts(q_p, kv_kp, kv_vp)
    mix_p = from_heads_t(moba_attn_prompt(qt, k4, vt, _moba_prompt_bias(table) * LOG2E))

    cache_kt = cache_k.transpose(0, 2, 3, 1)
    cache_vt = cache_v.transpose(0, 2, 3, 1)
    q_cols = q_s.reshape(bs, N_HEADS, HEAD_DIM).transpose(0, 2, 1)
    group_mask = jnp.asarray((np.arange(N_KV_HEADS)[:, None, None] == np.arange(N_HEADS)[None, None, :] // GROUP)
                             .astype(np.float32))
    gates_s = page_block_gates(page_table, cache_kt, q_cols[:, None] * group_mask, n_full)
    sel = moba_select_sample(gates_s, n_sel).transpose(0, 2, 1)
    ppb = MOBA_BLOCK // PAGE_SIZE
    pages = sel[..., None] * ppb + jnp.arange(ppb, dtype=jnp.int32)
    phys = jnp.take_along_axis(page_table, pages.reshape(bs, -1), axis=1)
    tiles = jnp.clip(n_full - 1 - sel, 0, SAMPLE_FAR_TILE).reshape(bs, -1)
    to_cols = lambda t: t.reshape(bs, N_KV_HEADS, HEAD_DIM).transpose(0, 2, 1)
    o_cols = moba_attn_sample(phys, tiles, cache_kt, cache_vt, q_cols, to_cols(kv_ks), to_cols(kv_vs),
                              _moba_sample_bias(table, past_len), table[0][None, :], n_sel)
    mix_s = o_cols.transpose(0, 2, 1)

    w_o = cast(w_o_b[0])
    yp = matmul_res(mix_p, w_o, yp, tm_p)
    ys = matmul_res(mix_s.reshape(bs, hq), w_o, ys, tm_s)

    w_r = cast(w_router[0])
    xn = jnp.zeros((mp + bs, d), F32)
    xn, gates_p, idx_p = moe_router(yp, ffn_norm_g[1], w_r, tm_p, xn, 0)
    xn, gates_s, idx_s = moe_router(ys, ffn_norm_g[1], w_r, tm_s, xn, mp)
    idx = jnp.concatenate([idx_p, idx_s], axis=0)
    tm_e = 768
    f_moe = w_gate_moe.shape[3]
    tn_e = _tile(f_moe, 512) if f_moe % 128 == 0 else f_moe
    n_chunks = 4
    row_src, slot, tile_expert, tile_live = moe_dispatch_plan(idx, tm_e, n_chunks)
    tiles_per_chunk = tile_expert.shape[0] // n_chunks
    e_sorted = jnp.zeros((row_src.shape[0], d), F32)
    for c in range(n_chunks):
        tiles = slice(c * tiles_per_chunk, (c + 1) * tiles_per_chunk)
        rows = slice(c * tiles_per_chunk * tm_e, (c + 1) * tiles_per_chunk * tm_e)
        x_chunk = jnp.take(xn, row_src[rows], axis=0, mode="clip")
        e_sorted = moe_ffn(tile_expert[tiles], tile_live[tiles], x_chunk, w_gate_moe[0], w_up_moe[0],
                           w_down_moe[0], e_sorted, c * tiles_per_chunk, tm_e, tn_e)
    def combine(y, gates, slots, tm):
        e1 = jnp.take(e_sorted, slots[:, 0], axis=0, mode="clip")
        e2 = jnp.take(e_sorted, slots[:, 1], axis=0, mode="clip")
        return combine_norm(y, e1, e2, gates, final_norm_g, tm)

    y_prompt = combine(yp, gates_p, slot[:mp], tm_p).reshape(bp, sp, d)
    y_sample = combine(ys, gates_s, slot[mp:], tm_s).reshape(bs, ss, d)
    w_keep = min(WINDOW, sp)
    kv4 = lambda t, b, s: t.reshape(b, s, N_KV_HEADS, HEAD_DIM)
    win_k_prompt = kv4(kp, bp, sp)[:, None, sp - w_keep:]
    win_v_prompt = kv4(vp, bp, sp)[:, None, sp - w_keep:]
    win_k_sample = jnp.concatenate([state_win_k[:, 0], kv4(ks_new, bs, 1)], axis=1)[:, None, ss:]
    win_v_sample = jnp.concatenate([state_win_v[:, 0], kv4(vs_new, bs, 1)], axis=1)[:, None, ss:]
    return (y_prompt, y_sample, win_k_prompt, win_v_prompt,
            kv4(kv_kp, bp, sp), kv4(kv_vp, bp, sp),
            win_k_sample, win_v_sample,
            kv4(kv_ks, bs, 1), kv4(kv_vs, bs, 1))
```

```python
import functools
import math

import numpy as np
import jax
import jax.numpy as jnp
from jax import lax
from jax.experimental import pallas as pl
from jax.experimental.pallas import tpu as pltpu

N_HEADS = 16
N_KV_HEADS = 4
HEAD_DIM = 64
GROUP = N_HEADS // N_KV_HEADS
WINDOW = 128
BAND_BLOCK = 128
MOBA_BLOCK = 256
MOBA_TOPK = 3
MOBA_Q_BLOCK = 128
PAGE_SIZE = 128
N_BUCKETS = 32
REL_MAX_DIST = 1024
N_EXPERTS = 8
EXPERT_TOPK = 2
RMS_EPS = 1e-5
NEG_INF = -1e30
ATTN_SCALE = HEAD_DIM ** -0.5
LOG2E = math.log2(math.e)
QCOLS = GROUP * MOBA_Q_BLOCK
FAR_DIST = 790
MOBA_FAR_TILE = 9
SAMPLE_FAR_TILE = 4
SUBLANES = 8

F32 = jnp.float32
BF16 = jnp.bfloat16
VMEM_LIMIT = 56 * 1024 * 1024


def _bucket_np(dist):
    max_exact = N_BUCKETS // 2
    n = np.maximum(np.asarray(dist, np.int64), 0)
    nf = np.maximum(n, 1).astype(np.float64)
    large = max_exact + (np.log(nf / max_exact) / math.log(REL_MAX_DIST / max_exact)
                         * (N_BUCKETS - max_exact)).astype(np.int64)
    return np.where(n < max_exact, n, np.minimum(large, N_BUCKETS - 1)).astype(np.int32)


def _tile(n, pref):
    if n <= pref:
        return n
    for t in range(pref, 7, -8):
        if n % t == 0:
            return t
    return n


def _params(**kw):
    return pltpu.CompilerParams(vmem_limit_bytes=VMEM_LIMIT, **kw)


def _rms(x, g):
    return (x * lax.rsqrt(jnp.mean(x * x, axis=-1, keepdims=True) + RMS_EPS)) * g


def _rms_matmul_kernel(x_ref, g_ref, w_ref, *o_refs):
    xn = _rms(x_ref[...], g_ref[...]).astype(BF16)
    z = jnp.dot(xn, w_ref[...], preferred_element_type=F32)
    col = 0
    for o_ref in o_refs:
        o_ref[...] = z[:, col:col + o_ref.shape[1]]
        col += o_ref.shape[1]


def rms_matmul(x, g, w, tm, widths=None):
    m, d = x.shape
    n = w.shape[1]
    widths = widths or (n,)
    assert sum(widths) == n
    outs = pl.pallas_call(
        _rms_matmul_kernel,
        grid=(m // tm,),
        in_specs=[pl.BlockSpec((tm, d), lambda i: (i, 0)),
                  pl.BlockSpec((1, d), lambda i: (0, 0)),
                  pl.BlockSpec((d, n), lambda i: (0, 0))],
        out_specs=[pl.BlockSpec((tm, wd), lambda i: (i, 0)) for wd in widths],
        out_shape=[jax.ShapeDtypeStruct((m, wd), F32) for wd in widths],
        compiler_params=_params(),
        name="rms_matmul",
    )(x, g.reshape(1, d), w)
    return outs if len(widths) > 1 else outs[0]


def _matmul_res_kernel(a_ref, w_ref, r_ref, o_ref):
    o_ref[...] = r_ref[...] + jnp.dot(a_ref[...].astype(BF16), w_ref[...], preferred_element_type=F32)


def matmul_res(a, w, res, tm):
    m, k = a.shape
    n = w.shape[1]
    return pl.pallas_call(
        _matmul_res_kernel,
        grid=(m // tm,),
        in_specs=[pl.BlockSpec((tm, k), lambda i: (i, 0)),
                  pl.BlockSpec((k, n), lambda i: (0, 0)),
                  pl.BlockSpec((tm, n), lambda i: (i, 0))],
        out_specs=pl.BlockSpec((tm, n), lambda i: (i, 0)),
        out_shape=jax.ShapeDtypeStruct((m, n), F32),
        compiler_params=_params(),
        name="matmul_res",
    )(a, w, res)


def _ffn_kernel(x_ref, g_ref, wg_ref, wu_ref, wd_ref, o_ref, xn_ref, acc_ref):
    j = pl.program_id(1)

    @pl.when(j == 0)
    def _():
        xn_ref[...] = _rms(x_ref[...], g_ref[...]).astype(BF16)
        acc_ref[...] = jnp.zeros_like(acc_ref)

    xn = xn_ref[...]
    a = jnp.dot(xn, wg_ref[...], preferred_element_type=F32)
    b = jnp.dot(xn, wu_ref[...], preferred_element_type=F32)
    h = (a * jax.nn.sigmoid(a)) * b
    acc_ref[...] += jnp.dot(h.astype(BF16), wd_ref[...], preferred_element_type=F32)

    @pl.when(j == pl.num_programs(1) - 1)
    def _():
        o_ref[...] = x_ref[...] + acc_ref[...]


def ffn_dense(x, g, wg, wu, wd, tm, tn):
    m, d = x.shape
    f = wg.shape[1]
    return pl.pallas_call(
        _ffn_kernel,
        grid=(m // tm, f // tn),
        in_specs=[pl.BlockSpec((tm, d), lambda i, j: (i, 0)),
                  pl.BlockSpec((1, d), lambda i, j: (0, 0)),
                  pl.BlockSpec((d, tn), lambda i, j: (0, j)),
                  pl.BlockSpec((d, tn), lambda i, j: (0, j)),
                  pl.BlockSpec((tn, d), lambda i, j: (j, 0))],
        out_specs=pl.BlockSpec((tm, d), lambda i, j: (i, 0)),
        out_shape=jax.ShapeDtypeStruct((m, d), F32),
        scratch_shapes=[pltpu.VMEM((tm, d), BF16), pltpu.VMEM((tm, d), F32)],
        compiler_params=_params(),
        name="ffn_dense",
    )(x, g.reshape(1, d), wg, wu, wd)


def _heads_to_cols(q_rows):
    qt = q_rows.T
    return jnp.concatenate([qt[r * HEAD_DIM:(r + 1) * HEAD_DIM] for r in range(GROUP)], axis=1)


def _cols_to_heads(o, n_q):
    ot = jnp.concatenate([o[:, r * n_q:(r + 1) * n_q] for r in range(GROUP)], axis=0)
    return ot.T


def _win_prompt_kernel(q_ref, kp_ref, ko_ref, vp_ref, vo_ref, wb_ref, sink_ref, o_ref):
    n = pl.program_id(1)
    row = lax.broadcasted_iota(jnp.int32, (2 * BAND_BLOCK, QCOLS), 0)
    has_prev = (row >= BAND_BLOCK) | (n > 0)
    gw = GROUP * HEAD_DIM
    for g in range(N_KV_HEADS):
        q = (_heads_to_cols(q_ref[:, g * gw:(g + 1) * gw]) * ATTN_SCALE).astype(BF16)
        k = jnp.concatenate([kp_ref[0, g], ko_ref[0, g]], axis=0).astype(BF16)
        s = jnp.dot(k, q, preferred_element_type=F32) + wb_ref[g]
        s = jnp.where(has_prev, s, NEG_INF)
        sink = sink_ref[g]
        m = jnp.maximum(jnp.max(s, axis=0, keepdims=True), sink)
        e = jnp.exp(s - m)
        den = jnp.sum(e, axis=0, keepdims=True) + jnp.exp(sink - m)
        vt = jnp.concatenate([vp_ref[0, g], vo_ref[0, g]], axis=1).astype(BF16)
        o = jnp.dot(vt, e.astype(BF16), preferred_element_type=F32) / den
        o_ref[:, g * gw:(g + 1) * gw] = _cols_to_heads(o, BAND_BLOCK).astype(o_ref.dtype)


def window_attn_prompt(q, k4, vt, wb, sink_cols):
    b, _, s, _ = k4.shape
    hq = q.shape[1]
    nb = s // BAND_BLOCK
    qb = BAND_BLOCK
    prev = lambda n: jnp.maximum(n - 1, 0)
    return pl.pallas_call(
        _win_prompt_kernel,
        grid=(b, nb),
        in_specs=[pl.BlockSpec((qb, hq), lambda b, n: (b * nb + n, 0)),
                  pl.BlockSpec((1, N_KV_HEADS, qb, HEAD_DIM), lambda b, n: (b, 0, prev(n), 0)),
                  pl.BlockSpec((1, N_KV_HEADS, qb, HEAD_DIM), lambda b, n: (b, 0, n, 0)),
                  pl.BlockSpec((1, N_KV_HEADS, HEAD_DIM, qb), lambda b, n: (b, 0, 0, prev(n))),
                  pl.BlockSpec((1, N_KV_HEADS, HEAD_DIM, qb), lambda b, n: (b, 0, 0, n)),
                  pl.BlockSpec((N_KV_HEADS, 2 * qb, QCOLS), lambda b, n: (0, 0, 0)),
                  pl.BlockSpec((N_KV_HEADS, 1, QCOLS), lambda b, n: (0, 0, 0))],
        out_specs=pl.BlockSpec((qb, hq), lambda b, n: (b * nb + n, 0)),
        out_shape=jax.ShapeDtypeStruct(q.shape, BF16),
        compiler_params=_params(),
        name="window_attn_prompt",
    )(q, k4, k4, vt, vt, wb, sink_cols)


def _moba_prompt_kernel(q_ref, k_ref, vt_ref, tb_ref, o_ref, kmean_ref, sel_ref, *, n_blk):
    qb = pl.program_id(2)
    tiles_per_blk = MOBA_BLOCK // MOBA_Q_BLOCK

    @pl.when(qb == 0)
    def _():
        kf = k_ref[0, 0].reshape(n_blk, MOBA_BLOCK, HEAD_DIM)
        kmean_ref[...] = jnp.mean(kf, axis=1)

    kmean = kmean_ref[...].astype(BF16)
    q_all = _heads_to_cols(q_ref[...])
    qs = []
    for tt in range(tiles_per_blk):
        q32 = jnp.concatenate([q_all[:, r * MOBA_BLOCK + tt * MOBA_Q_BLOCK:
                                     r * MOBA_BLOCK + (tt + 1) * MOBA_Q_BLOCK]
                               for r in range(GROUP)], axis=1)
        gate = jnp.dot(kmean, q32.astype(BF16), preferred_element_type=F32)
        blk = lax.broadcasted_iota(jnp.int32, gate.shape, 0)
        past = blk < qb
        gate = jnp.where(past, gate, NEG_INF)
        rank = jnp.zeros(gate.shape, jnp.int32)
        for kk in range(n_blk):
            gk = gate[kk:kk + 1, :]
            beats = (gk > gate) | ((gk == gate) & (kk < blk))
            rank = rank + beats.astype(jnp.int32)
        sel = (past & (rank < MOBA_TOPK)) | (blk == qb)
        sel_ref[tt] = sel.astype(F32)
        qs.append((q32 * (ATTN_SCALE * LOG2E)).astype(BF16))

    def masked_logits(tt, s, j):
        u = jnp.clip(qb * tiles_per_blk + tt - tiles_per_blk * j, 0, MOBA_FAR_TILE)
        return jnp.where(sel_ref[tt, pl.ds(j, 1), :] > 0, s + tb_ref[0, u], NEG_INF)

    def body(jp, carry):
        off = pl.multiple_of(jp * (2 * MOBA_BLOCK), 2 * MOBA_BLOCK)
        kj = k_ref[0, 0, pl.ds(off, 2 * MOBA_BLOCK), :].astype(BF16)
        vj = vt_ref[0, 0, :, pl.ds(off, 2 * MOBA_BLOCK)].astype(BF16)
        out = []
        for tt in range(tiles_per_blk):
            m, l, acc = carry[3 * tt:3 * tt + 3]
            s = jnp.dot(kj, qs[tt], preferred_element_type=F32)
            s0 = masked_logits(tt, s[:MOBA_BLOCK], 2 * jp)
            s1 = masked_logits(tt, s[MOBA_BLOCK:], 2 * jp + 1)
            m_new = jnp.maximum(m, jnp.maximum(jnp.max(s0, axis=0, keepdims=True),
                                               jnp.max(s1, axis=0, keepdims=True)))
            alpha = jnp.exp2(m - m_new)
            p0 = jnp.exp2(s0 - m_new)
            p1 = jnp.exp2(s1 - m_new)
            l = alpha * l + (jnp.sum(p0, axis=0, keepdims=True) + jnp.sum(p1, axis=0, keepdims=True))
            acc = (alpha * acc
                   + jnp.dot(vj[:, :MOBA_BLOCK], p0.astype(BF16), preferred_element_type=F32)
                   + jnp.dot(vj[:, MOBA_BLOCK:], p1.astype(BF16), preferred_element_type=F32))
            out += [m_new, l, acc]
        return tuple(out)

    init = (jnp.full((1, QCOLS), NEG_INF, F32), jnp.zeros((1, QCOLS), F32),
            jnp.zeros((HEAD_DIM, QCOLS), F32)) * tiles_per_blk
    final = lax.fori_loop(0, qb // 2 + 1, body, init)
    outs = [final[3 * tt + 2] / final[3 * tt + 1] for tt in range(tiles_per_blk)]
    o_all = jnp.concatenate([o[:, r * MOBA_Q_BLOCK:(r + 1) * MOBA_Q_BLOCK]
                             for r in range(GROUP) for o in outs], axis=1)
    o_ref[...] = _cols_to_heads(o_all, MOBA_BLOCK).astype(o_ref.dtype)


def moba_attn_prompt(q, k4, vt, tb):
    b, _, s, _ = k4.shape
    gw = GROUP * HEAD_DIM
    n_blk = s // MOBA_BLOCK
    assert n_blk % 2 == 0, "kv blocks are swept in pairs"
    n_tb = tb.shape[1]
    tiles = MOBA_BLOCK // MOBA_Q_BLOCK
    return pl.pallas_call(
        functools.partial(_moba_prompt_kernel, n_blk=n_blk),
        grid=(N_KV_HEADS, b, n_blk),
        in_specs=[pl.BlockSpec((MOBA_BLOCK, gw), lambda g, b, t: (b * n_blk + t, g)),
                  pl.BlockSpec((1, 1, s, HEAD_DIM), lambda g, b, t: (b, g, 0, 0)),
                  pl.BlockSpec((1, 1, HEAD_DIM, s), lambda g, b, t: (b, g, 0, 0)),
                  pl.BlockSpec((1, n_tb, MOBA_BLOCK, QCOLS), lambda g, b, t: (g, 0, 0, 0))],
        out_specs=pl.BlockSpec((MOBA_BLOCK, gw), lambda g, b, t: (b * n_blk + t, g)),
        out_shape=jax.ShapeDtypeStruct(q.shape, BF16),
        scratch_shapes=[pltpu.VMEM((n_blk, HEAD_DIM), F32),
                        pltpu.VMEM((tiles, n_blk, QCOLS), F32)],
        compiler_params=_params(),
        name="moba_attn_prompt",
    )(q, k4, vt, tb)


def _expand_heads(q, mask):
    return jnp.concatenate([q] * N_KV_HEADS, axis=1) * mask


def _fold_heads(o):
    acc = o
    for c in range(1, N_KV_HEADS):
        acc = acc + pltpu.roll(o, c * HEAD_DIM, axis=1)
    return acc[:, :HEAD_DIM]


def _win_sample_kernel(q_ref, kn_ref, vn_ref, kb_ref, vb_ref, bias_ref, bias0_ref, sink_ref,
                       mask_ref, o_ref, *, bb):
    mask = mask_ref[...]
    sink = sink_ref[...]
    rnd = lambda x: x.astype(BF16).astype(F32)
    for i in range(bb):
        qe = _expand_heads(q_ref[i] * ATTN_SCALE, mask).astype(BF16)
        s = lax.dot_general(qe, kb_ref[i].astype(BF16), (((1,), (1,)), ((), ())),
                            preferred_element_type=F32) + bias_ref[...]
        s0 = jnp.sum(qe.astype(F32) * rnd(kn_ref[i]), axis=1, keepdims=True) + bias0_ref[...]
        m = jnp.maximum(jnp.maximum(jnp.max(s, axis=1, keepdims=True), s0), sink)
        e = jnp.exp(s - m)
        e0 = jnp.exp(s0 - m)
        inv = 1.0 / (jnp.sum(e, axis=1, keepdims=True) + e0 + jnp.exp(sink - m))
        o = (jnp.dot((e * inv).astype(BF16), vb_ref[i].astype(BF16), preferred_element_type=F32)
             + rnd(e0 * inv) * rnd(vn_ref[i]))
        o_ref[i] = _fold_heads(o * mask)


def window_attn_sample(q3, k_new, v_new, k_buf, v_buf, bias, bias0, sink, mask):
    db, w, kvw = k_buf.shape
    bb = _tile(db, 8)
    row = lambda i: (i, 0, 0)
    fixed = lambda i: (0, 0)
    return pl.pallas_call(
        functools.partial(_win_sample_kernel, bb=bb),
        grid=(db // bb,),
        in_specs=[pl.BlockSpec((bb, N_HEADS, HEAD_DIM), row),
                  pl.BlockSpec((bb, 1, kvw), row),
                  pl.BlockSpec((bb, 1, kvw), row),
                  pl.BlockSpec((bb, w, kvw), row),
                  pl.BlockSpec((bb, w, kvw), row),
                  pl.BlockSpec((N_HEADS, w), fixed),
                  pl.BlockSpec((N_HEADS, 1), fixed),
                  pl.BlockSpec((N_HEADS, 1), fixed),
                  pl.BlockSpec((N_HEADS, kvw), fixed)],
        out_specs=pl.BlockSpec((bb, N_HEADS, HEAD_DIM), row),
        out_shape=jax.ShapeDtypeStruct((db, N_HEADS, HEAD_DIM), F32),
        compiler_params=_params(),
        name="window_attn_sample",
    )(q3, k_new, v_new, k_buf, v_buf, bias, bias0, sink, mask)


def _copy_page(cache_ref, page, dst_ref, sem):
    return pltpu.make_async_copy(cache_ref.at[page], dst_ref, sem)


def _page_gate_kernel(pt_ref, cache_ref, qm_ref, o_ref, buf_ref, sem_ref, *, cpp, chunks_per_b):
    step = pl.program_id(0)
    n_steps = pl.num_programs(0)

    def copies(s, slot):
        b = s // chunks_per_b
        c = s % chunks_per_b
        return [_copy_page(cache_ref, pt_ref[b, c * cpp + p], buf_ref.at[slot, p], sem_ref.at[slot])
                for p in range(cpp)]

    @pl.when(step == 0)
    def _():
        for cp in copies(step, 0):
            cp.start()

    @pl.when(step + 1 < n_steps)
    def _():
        for cp in copies(step + 1, (step + 1) % 2):
            cp.start()

    slot = step % 2
    for cp in copies(step, slot):
        cp.wait()
    ppb = MOBA_BLOCK // PAGE_SIZE
    qm = qm_ref[0].astype(BF16).astype(F32)
    for n in range(cpp // ppb):
        x = jnp.sum(buf_ref[slot, pl.ds(n * ppb, ppb)], axis=0)
        mean = jnp.sum(x, axis=-1, keepdims=True) * (1.0 / MOBA_BLOCK)
        mean = mean.astype(BF16).astype(F32)
        o_ref[0, n:n + 1, :] = jnp.sum(jnp.sum(mean * qm, axis=0), axis=0, keepdims=True)


def page_block_gates(page_table, cache_t, qm, n_full):
    db = page_table.shape[0]
    ppb = MOBA_BLOCK // PAGE_SIZE
    n_pages = n_full * ppb
    cpp = _tile(n_pages, 64)
    chunks_per_b = n_pages // cpp
    return pl.pallas_call(
        functools.partial(_page_gate_kernel, cpp=cpp, chunks_per_b=chunks_per_b),
        grid_spec=pltpu.PrefetchScalarGridSpec(
            num_scalar_prefetch=1,
            grid=(db * chunks_per_b,),
            in_specs=[pl.BlockSpec(memory_space=pl.ANY),
                      pl.BlockSpec((1, N_KV_HEADS, HEAD_DIM, N_HEADS),
                                   lambda s, pt: (s // chunks_per_b, 0, 0, 0))],
            out_specs=pl.BlockSpec((1, cpp // ppb, N_HEADS),
                                   lambda s, pt: (s // chunks_per_b, s % chunks_per_b, 0)),
            scratch_shapes=[pltpu.VMEM((2, cpp, N_KV_HEADS, HEAD_DIM, PAGE_SIZE), F32),
                            pltpu.SemaphoreType.DMA((2,))]),
        out_shape=jax.ShapeDtypeStruct((db, n_full, N_HEADS), F32),
        compiler_params=_params(dimension_semantics=("arbitrary",)),
        name="page_block_gates",
    )(page_table, cache_t, qm)


def _moba_select_kernel(gate_ref, sel_ref, *, bb, n_sel):
    for i in range(bb):
        gate = gate_ref[i]
        n_full = gate.shape[0]
        row = lax.broadcasted_iota(jnp.int32, gate.shape, 0).astype(F32)
        picks = []
        for _ in range(n_sel):
            best = jnp.max(gate, axis=0, keepdims=True)
            idx = jnp.min(jnp.where(gate == best, row, float(n_full)), axis=0, keepdims=True)
            picks.append(idx)
            gate = jnp.where(row == idx, -jnp.inf, gate)
        sel_ref[i] = jnp.concatenate(picks, axis=0).astype(jnp.int32)


def moba_select_sample(gates, n_sel):
    db, n_full, _ = gates.shape
    bb = _tile(db, 8)
    return pl.pallas_call(
        functools.partial(_moba_select_kernel, bb=bb, n_sel=n_sel),
        grid=(db // bb,),
        in_specs=[pl.BlockSpec((bb, n_full, N_HEADS), lambda i: (i, 0, 0))],
        out_specs=pl.BlockSpec((bb, n_sel, N_HEADS), lambda i: (i, 0, 0)),
        out_shape=jax.ShapeDtypeStruct((db, n_sel, N_HEADS), jnp.int32),
        compiler_params=_params(),
        name="moba_select_sample",
    )(gates)


def _moba_sample_kernel(phys_ref, tile_ref, ck_ref, cv_ref, q_ref, kn_ref, vn_ref, bt_ref, bias0_ref,
                        o_ref, kbuf_ref, vbuf_ref, s_ref, p_ref, sem_ref, *, n_sel):
    b = pl.program_id(0)
    n_b = pl.num_programs(0)
    ppb = MOBA_BLOCK // PAGE_SIZE
    n_pg = n_sel * ppb

    def copies(bi, slot):
        out = []
        for h in range(N_HEADS):
            g = h // GROUP
            for p in range(n_pg):
                page = phys_ref[bi, h * n_pg + p]
                out.append(pltpu.make_async_copy(ck_ref.at[page, g], kbuf_ref.at[slot, h, p],
                                                 sem_ref.at[0, slot]))
                out.append(pltpu.make_async_copy(cv_ref.at[page, g], vbuf_ref.at[slot, h, p],
                                                 sem_ref.at[1, slot]))
        return out

    @pl.when(b == 0)
    def _():
        for cp in copies(b, 0):
            cp.start()

    @pl.when(b + 1 < n_b)
    def _():
        for cp in copies(b + 1, (b + 1) % 2):
            cp.start()

    slot = b % 2
    for cp in copies(b, slot):
        cp.wait()

    rnd = lambda x: x.astype(BF16).astype(F32)
    n_rows = s_ref.shape[1]
    lane = lax.broadcasted_iota(jnp.int32, (1, PAGE_SIZE), 1)
    if n_rows > n_pg + 1:
        s_ref[:, n_pg + 1:, :] = jnp.full((N_HEADS, n_rows - n_pg - 1, PAGE_SIZE), NEG_INF, F32)
    for h in range(N_HEADS):
        g = h // GROUP
        q = rnd(q_ref[0, :, h:h + 1] * ATTN_SCALE)
        for si in range(n_sel):
            tile = tile_ref[b, h * n_sel + si]
            for pp in range(ppb):
                p = si * ppb + pp
                s_ref[h, p:p + 1, :] = (jnp.sum(rnd(kbuf_ref[slot, h, p]) * q, axis=0, keepdims=True)
                                        + bt_ref[tile, h, pp:pp + 1, :])
        s0 = (jnp.sum(q * rnd(kn_ref[0, :, g:g + 1]), axis=0, keepdims=True)
              + bias0_ref[:, h:h + 1])
        s_ref[h, n_pg:n_pg + 1, :] = jnp.where(lane == 0, s0, NEG_INF)
    s = s_ref[...]
    m = jnp.max(jnp.max(s, axis=2, keepdims=True), axis=1, keepdims=True)
    e = jnp.exp(s - m)
    den = jnp.sum(jnp.sum(e, axis=2, keepdims=True), axis=1, keepdims=True)
    p_ref[...] = rnd(e * (1.0 / den))
    head = lax.broadcasted_iota(jnp.int32, (1, N_HEADS), 1)
    o_all = jnp.zeros((HEAD_DIM, N_HEADS), F32)
    for h in range(N_HEADS):
        g = h // GROUP
        acc = rnd(vbuf_ref[slot, h, 0]) * p_ref[h, 0:1, :]
        for pg in range(1, n_pg):
            acc = acc + rnd(vbuf_ref[slot, h, pg]) * p_ref[h, pg:pg + 1, :]
        o = (jnp.sum(acc, axis=1, keepdims=True)
             + p_ref[h, n_pg:n_pg + 1, 0:1] * rnd(vn_ref[0, :, g:g + 1]))
        o_all = jnp.where(head == h, o, o_all)
    o_ref[0] = o_all


def moba_attn_sample(phys, tiles, cache_kt, cache_vt, q_cols, k_new, v_new, bias_tiles, bias0, n_sel):
    db = q_cols.shape[0]
    n_pg = n_sel * (MOBA_BLOCK // PAGE_SIZE)
    n_rows = -(-(n_pg + 1) // SUBLANES) * SUBLANES
    row = lambda b, *_: (b, 0, 0)
    return pl.pallas_call(
        functools.partial(_moba_sample_kernel, n_sel=n_sel),
        grid_spec=pltpu.PrefetchScalarGridSpec(
            num_scalar_prefetch=2,
            grid=(db,),
            in_specs=[pl.BlockSpec(memory_space=pl.ANY),
                      pl.BlockSpec(memory_space=pl.ANY),
                      pl.BlockSpec((1, HEAD_DIM, N_HEADS), row),
                      pl.BlockSpec((1, HEAD_DIM, N_KV_HEADS), row),
                      pl.BlockSpec((1, HEAD_DIM, N_KV_HEADS), row),
                      pl.BlockSpec(bias_tiles.shape, lambda b, *_: (0, 0, 0, 0)),
                      pl.BlockSpec((1, N_HEADS), lambda b, *_: (0, 0))],
            out_specs=pl.BlockSpec((1, HEAD_DIM, N_HEADS), row),
            scratch_shapes=[pltpu.VMEM((2, N_HEADS, n_pg, HEAD_DIM, PAGE_SIZE), F32),
                            pltpu.VMEM((2, N_HEADS, n_pg, HEAD_DIM, PAGE_SIZE), F32),
                            pltpu.VMEM((N_HEADS, n_rows, PAGE_SIZE), F32),
                            pltpu.VMEM((N_HEADS, n_rows, PAGE_SIZE), F32),
                            pltpu.SemaphoreType.DMA((2, 2))]),
        out_shape=jax.ShapeDtypeStruct((db, HEAD_DIM, N_HEADS), F32),
        compiler_params=_params(dimension_semantics=("arbitrary",)),
        name="moba_attn_sample",
    )(phys, tiles, cache_kt, cache_vt, q_cols, k_new, v_new, bias_tiles, bias0)


def _router_kernel(x_ref, g_ref, wr_ref, xn_ref, gate_ref, idx_ref):
    xn = _rms(x_ref[...], g_ref[...])
    xn_ref[...] = xn
    logits = jnp.dot(xn.astype(BF16), wr_ref[...], preferred_element_type=F32)
    n_e = logits.shape[1]
    col = lax.broadcasted_iota(jnp.int32, logits.shape, 1).astype(F32)
    m1 = jnp.max(logits, axis=1, keepdims=True)
    i1 = jnp.min(jnp.where(logits == m1, col, float(n_e)), axis=1, keepdims=True)
    rest = jnp.where(col == i1, -jnp.inf, logits)
    m2 = jnp.max(rest, axis=1, keepdims=True)
    i2 = jnp.min(jnp.where(rest == m2, col, float(n_e)), axis=1, keepdims=True)
    e2 = jnp.exp(m2 - m1)
    den = 1.0 + e2
    gate_ref[...] = jnp.concatenate([1.0 / den, e2 / den], axis=1)
    idx_ref[...] = jnp.concatenate([i1, i2], axis=1).astype(jnp.int32)


def _router_into_kernel(x_ref, g_ref, wr_ref, prev_ref, xn_ref, gate_ref, idx_ref):
    del prev_ref
    _router_kernel(x_ref, g_ref, wr_ref, xn_ref, gate_ref, idx_ref)


def moe_router(x, g, w_router, tm, xn_all, row0):
    m, d = x.shape
    n_e = w_router.shape[1]
    assert row0 % tm == 0
    first = row0 // tm
    return pl.pallas_call(
        _router_into_kernel,
        grid=(m // tm,),
        in_specs=[pl.BlockSpec((tm, d), lambda i: (i, 0)),
                  pl.BlockSpec((1, d), lambda i: (0, 0)),
                  pl.BlockSpec((d, n_e), lambda i: (0, 0)),
                  pl.BlockSpec(memory_space=pl.ANY)],
        out_specs=[pl.BlockSpec((tm, d), lambda i: (first + i, 0)),
                   pl.BlockSpec((tm, EXPERT_TOPK), lambda i: (i, 0)),
                   pl.BlockSpec((tm, EXPERT_TOPK), lambda i: (i, 0))],
        out_shape=[jax.ShapeDtypeStruct(xn_all.shape, F32),
                   jax.ShapeDtypeStruct((m, EXPERT_TOPK), F32),
                   jax.ShapeDtypeStruct((m, EXPERT_TOPK), jnp.int32)],
        input_output_aliases={3: 0},
        compiler_params=_params(),
        name="moe_router",
    )(x, g.reshape(1, d), w_router, xn_all)


def _moe_ffn_kernel(te_ref, live_ref, x_ref, wg_ref, wu_ref, wd_ref, prev_ref, o_ref, xb_ref, acc_ref):
    del prev_ref
    i = pl.program_id(0)
    j = pl.program_id(1)

    @pl.when(j == 0)
    def _():
        xb_ref[...] = x_ref[...].astype(BF16)
        acc_ref[...] = jnp.zeros_like(acc_ref)

    @pl.when(live_ref[i] > 0)
    def _():
        x = xb_ref[...]
        a = jnp.dot(x, wg_ref[0].astype(BF16), preferred_element_type=F32)
        b = jnp.dot(x, wu_ref[0].astype(BF16), preferred_element_type=F32)
        h = (a * jax.nn.sigmoid(a)) * b
        acc_ref[...] += jnp.dot(h.astype(BF16), wd_ref[0].astype(BF16), preferred_element_type=F32)

    @pl.when(j == pl.num_programs(1) - 1)
    def _():
        o_ref[...] = acc_ref[...]


def moe_ffn(tile_expert, tile_live, x_sorted, wg, wu, wd, e_sorted, first_tile, tm, tn):
    n_tiles = tile_expert.shape[0]
    d = x_sorted.shape[1]
    f = wg.shape[2]
    return pl.pallas_call(
        _moe_ffn_kernel,
        grid_spec=pltpu.PrefetchScalarGridSpec(
            num_scalar_prefetch=2,
            grid=(n_tiles, f // tn),
            in_specs=[pl.BlockSpec((tm, d), lambda i, j, te, lv: (i, 0)),
                      pl.BlockSpec((1, d, tn), lambda i, j, te, lv: (te[i], 0, j)),
                      pl.BlockSpec((1, d, tn), lambda i, j, te, lv: (te[i], 0, j)),
                      pl.BlockSpec((1, tn, d), lambda i, j, te, lv: (te[i], j, 0)),
                      pl.BlockSpec(memory_space=pl.ANY)],
            out_specs=pl.BlockSpec((tm, d), lambda i, j, te, lv: (first_tile + i, 0)),
            scratch_shapes=[pltpu.VMEM((tm, d), BF16), pltpu.VMEM((tm, d), F32)]),
        out_shape=jax.ShapeDtypeStruct(e_sorted.shape, F32),
        input_output_aliases={6: 0},
        compiler_params=_params(),
        name="moe_ffn",
    )(tile_expert, tile_live, x_sorted, wg, wu, wd, e_sorted)


def _combine_norm_kernel(y_ref, e1_ref, e2_ref, gate_ref, g_ref, o_ref):
    gate = gate_ref[...]
    y = y_ref[...] + (gate[:, 0:1] * e1_ref[...] + gate[:, 1:2] * e2_ref[...])
    o_ref[...] = _rms(y, g_ref[...])


def combine_norm(y, e1, e2, gates, g, tm):
    m, d = y.shape
    blk = pl.BlockSpec((tm, d), lambda i: (i, 0))
    return pl.pallas_call(
        _combine_norm_kernel,
        grid=(m // tm,),
        in_specs=[blk, blk, blk,
                  pl.BlockSpec((tm, EXPERT_TOPK), lambda i: (i, 0)),
                  pl.BlockSpec((1, d), lambda i: (0, 0))],
        out_specs=blk,
        out_shape=jax.ShapeDtypeStruct((m, d), F32),
        compiler_params=_params(),
        name="combine_norm",
    )(y, e1, e2, gates, g.reshape(1, d))


def moe_dispatch_plan(idx, tm, n_chunks):
    m, k = idx.shape
    n_asg = m * k
    p = -(-(n_asg + N_EXPERTS * (tm - 1)) // (tm * n_chunks)) * (tm * n_chunks)
    flat = idx.reshape(n_asg)
    onehot = (flat[:, None] == jnp.arange(N_EXPERTS, dtype=jnp.int32)[None, :]).astype(jnp.int32)
    csum = jnp.cumsum(onehot, axis=0)
    count = csum[-1]
    rank = jnp.take_along_axis(csum, flat[:, None], axis=1)[:, 0] - 1
    padded = -(-count // tm) * tm
    pad_end = jnp.cumsum(padded)
    pad_start = pad_end - padded
    slot = pad_start[flat] + rank
    order = jnp.sort(flat * n_asg + jnp.arange(n_asg, dtype=jnp.int32)) % n_asg
    start = jnp.cumsum(count) - count
    slots = jnp.arange(p, dtype=jnp.int32)
    slot_e = jnp.minimum(jnp.searchsorted(pad_end, slots, side="right"), N_EXPERTS - 1).astype(jnp.int32)
    r = slots - pad_start[slot_e]
    live = r < count[slot_e]
    src_asg = order[jnp.clip(start[slot_e] + r, 0, n_asg - 1)]
    row_src = jnp.where(live, src_asg // k, 0).astype(jnp.int32)
    tile_expert = slot_e[::tm]
    tile_live = live[::tm].astype(jnp.int32)
    return row_src, slot.reshape(m, k), tile_expert, tile_live


def _toeplitz_offsets(rows, cols):
    x = np.arange(rows + cols)
    return np.where(x < cols, x, x - (rows + cols))


def _toeplitz(src, rows, cols):
    period = rows + cols
    lead = src.shape[:-1]
    flat = jnp.tile(src, (1,) * len(lead) + (rows,))[..., :rows * (period - 1)]
    return flat.reshape(*lead, rows, period - 1)[..., :cols]


def _distance_bias(table, dist, valid):
    bias = jnp.where(valid[..., None], table[_bucket_np(dist)], NEG_INF)
    return jnp.swapaxes(bias, -1, -2)


def _window_prompt_bias(table):
    rows, cols = 2 * BAND_BLOCK, BAND_BLOCK
    dist = BAND_BLOCK + _toeplitz_offsets(rows, cols)
    bias = _toeplitz(_distance_bias(table, dist, (dist >= 0) & (dist <= WINDOW)), rows, cols)
    bias = bias.reshape(N_KV_HEADS, GROUP, rows, cols).transpose(0, 2, 1, 3)
    return bias.reshape(N_KV_HEADS, rows, QCOLS)


def _moba_prompt_bias(table):
    rows, cols = MOBA_BLOCK, MOBA_Q_BLOCK
    n_t = MOBA_FAR_TILE + 1
    u = np.arange(n_t)[:, None]
    dist = u * MOBA_Q_BLOCK + _toeplitz_offsets(rows, cols)[None, :]
    dist = np.where(u == MOBA_FAR_TILE, np.maximum(dist, FAR_DIST), dist)
    bias = _toeplitz(_distance_bias(table, dist, dist >= 0), rows, cols)
    bias = bias.reshape(n_t, N_KV_HEADS, GROUP, rows, cols).transpose(1, 0, 3, 2, 4)
    return bias.reshape(N_KV_HEADS, n_t, rows, QCOLS)


def _moba_sample_bias(table, past_len):
    ti = np.arange(SAMPLE_FAR_TILE + 1)[:, None]
    r = np.arange(MOBA_BLOCK)[None, :]
    dist = (ti + 1) * MOBA_BLOCK - r + (past_len % MOBA_BLOCK)
    dist = np.where(ti == SAMPLE_FAR_TILE, np.maximum(dist, FAR_DIST), dist)
    bias = table[_bucket_np(dist)].transpose(0, 2, 1)
    return bias.reshape(SAMPLE_FAR_TILE + 1, N_HEADS, MOBA_BLOCK // PAGE_SIZE, PAGE_SIZE)


def kernel(x_prompt, x_sample, state_win_k, state_win_v, cache_k, cache_v, page_table, rel_bias_table,
           attn_norm_g, ffn_norm_g, w_qkv_a, sink_a, w_o_a, kv_norm_g, w_kv_shared, w_q_b, w_o_b,
           w_gate_dense, w_up_dense, w_down_dense, w_router, w_gate_moe, w_up_moe, w_down_moe, final_norm_g):
    bp, sp, d = x_prompt.shape
    bs, ss, _ = x_sample.shape
    assert ss == 1 and attn_norm_g.shape[0] == 2
    mp = bp * sp
    hq = N_HEADS * HEAD_DIM
    hk = N_KV_HEADS * HEAD_DIM
    past_len = page_table.shape[1] * PAGE_SIZE
    assert past_len % MOBA_BLOCK == 0 and sp % MOBA_BLOCK == 0
    n_full = past_len // MOBA_BLOCK
    n_sel = min(MOBA_TOPK, n_full)
    w_buf = state_win_k.shape[2]
    table = rel_bias_table.astype(F32)

    tm_p = _tile(mp, 512)
    tm_s = _tile(bs, 512)
    tn = _tile(w_gate_dense.shape[2], 896) if w_gate_dense.shape[2] % 128 == 0 else w_gate_dense.shape[2]
    cast = lambda w: w.astype(BF16)

    yp = x_prompt.reshape(mp, d)
    ys = x_sample.reshape(bs, d)

    w_qkv = cast(w_qkv_a[0])
    qp, kp, vp = rms_matmul(yp, attn_norm_g[0], w_qkv, tm_p, (hq, hk, hk))
    qs_new, ks_new, vs_new = rms_matmul(ys, attn_norm_g[0], w_qkv, tm_s, (hq, hk, hk))

    def kv_layouts(k, v):
        k4 = k.reshape(bp, sp, N_KV_HEADS, HEAD_DIM).transpose(0, 2, 1, 3)
        vt = v.reshape(bp, sp, N_KV_HEADS, HEAD_DIM).transpose(0, 2, 3, 1)
        return k4, vt

    k4, vt = kv_layouts(kp, vp)
    sink_cols = jnp.repeat(sink_a[0].astype(F32).reshape(N_KV_HEADS, 1, GROUP), BAND_BLOCK, axis=2)
    mix_p = window_attn_prompt(qp, k4, vt, _window_prompt_bias(table), sink_cols)

    head_mask = jnp.asarray((np.arange(N_HEADS)[:, None] // GROUP == np.arange(hk)[None, :] // HEAD_DIM)
                            .astype(np.float32))
    k_buf = state_win_k[:, 0].reshape(bs, w_buf, hk)
    v_buf = state_win_v[:, 0].reshape(bs, w_buf, hk)
    bias_ws = table[_bucket_np(w_buf - np.arange(w_buf))].T
    bias_0 = table[0][:, None]
    mix_s = window_attn_sample(qs_new.reshape(bs, N_HEADS, HEAD_DIM), ks_new[:, None], vs_new[:, None],
                               k_buf, v_buf, bias_ws, bias_0, sink_a[0].astype(F32)[:, None], head_mask)

    w_o = cast(w_o_a[0])
    yp = matmul_res(mix_p, w_o, yp, tm_p)
    ys = matmul_res(mix_s.reshape(bs, hq), w_o, ys, tm_s)

    wg, wu, wd = cast(w_gate_dense[0]), cast(w_up_dense[0]), cast(w_down_dense[0])
    yp = ffn_dense(yp, ffn_norm_g[0], wg, wu, wd, tm_p, tn)
    ys = ffn_dense(ys, ffn_norm_g[0], wg, wu, wd, tm_s, tn)

    w_kv = cast(w_kv_shared)
    kv_kp, kv_vp = rms_matmul(yp, kv_norm_g, w_kv, tm_p, (hk, hk))
    kv_ks, kv_vs = rms_matmul(ys, kv_norm_g, w_kv, tm_s, (hk, hk))
    w_q = cast(w_q_b[0])
    q_p = rms_matmul(yp, attn_norm_g[1], w_q, tm_p)
    q_s = rms_matmul(ys, attn_norm_g[1], w_q, tm_s)

    k4, vt = kv_layouts(kv_kp, kv_vp)
    mix_p = moba_attn_prompt(q_p, k4, vt, _moba_prompt_bias(table) * LOG2E)

    cache_kt = cache_k.transpose(0, 2, 3, 1)
    cache_vt = cache_v.transpose(0, 2, 3, 1)
    q_cols = q_s.reshape(bs, N_HEADS, HEAD_DIM).transpose(0, 2, 1)
    group_mask = jnp.asarray((np.arange(N_KV_HEADS)[:, None, None] == np.arange(N_HEADS)[None, None, :] // GROUP)
                             .astype(np.float32))
    gates_s = page_block_gates(page_table, cache_kt, q_cols[:, None] * group_mask, n_full)
    sel = moba_select_sample(gates_s, n_sel).transpose(0, 2, 1)
    ppb = MOBA_BLOCK // PAGE_SIZE
    pages = sel[..., None] * ppb + jnp.arange(ppb, dtype=jnp.int32)
    phys = jnp.take_along_axis(page_table, pages.reshape(bs, -1), axis=1)
    tiles = jnp.clip(n_full - 1 - sel, 0, SAMPLE_FAR_TILE).reshape(bs, -1)
    to_cols = lambda t: t.reshape(bs, N_KV_HEADS, HEAD_DIM).transpose(0, 2, 1)
    o_cols = moba_attn_sample(phys, tiles, cache_kt, cache_vt, q_cols, to_cols(kv_ks), to_cols(kv_vs),
                              _moba_sample_bias(table, past_len), table[0][None, :], n_sel)
    mix_s = o_cols.transpose(0, 2, 1)

    w_o = cast(w_o_b[0])
    yp = matmul_res(mix_p, w_o, yp, tm_p)
    ys = matmul_res(mix_s.reshape(bs, hq), w_o, ys, tm_s)

    w_r = cast(w_router[0])
    xn = jnp.zeros((mp + bs, d), F32)
    xn, gates_p, idx_p = moe_router(yp, ffn_norm_g[1], w_r, tm_p, xn, 0)
    xn, gates_s, idx_s = moe_router(ys, ffn_norm_g[1], w_r, tm_s, xn, mp)
    idx = jnp.concatenate([idx_p, idx_s], axis=0)
    tm_e = 768
    f_moe = w_gate_moe.shape[3]
    tn_e = _tile(f_moe, 512) if f_moe % 128 == 0 else f_moe
    n_chunks = 4
    row_src, slot, tile_expert, tile_live = moe_dispatch_plan(idx, tm_e, n_chunks)
    tiles_per_chunk = tile_expert.shape[0] // n_chunks
    e_sorted = jnp.zeros((row_src.shape[0], d), F32)
    for c in range(n_chunks):
        tiles = slice(c * tiles_per_chunk, (c + 1) * tiles_per_chunk)
        rows = slice(c * tiles_per_chunk * tm_e, (c + 1) * tiles_per_chunk * tm_e)
        x_chunk = jnp.take(xn, row_src[rows], axis=0, mode="clip")
        e_sorted = moe_ffn(tile_expert[tiles], tile_live[tiles], x_chunk, w_gate_moe[0], w_up_moe[0],
                           w_down_moe[0], e_sorted, c * tiles_per_chunk, tm_e, tn_e)
    def combine(y, gates, slots, tm):
        e1 = jnp.take(e_sorted, slots[:, 0], axis=0, mode="clip")
        e2 = jnp.take(e_sorted, slots[:, 1], axis=0, mode="clip")
        return combine_norm(y, e1, e2, gates, final_norm_g, tm)

    y_prompt = combine(yp, gates_p, slot[:mp], tm_p).reshape(bp, sp, d)
    y_sample = combine(ys, gates_s, slot[mp:], tm_s).reshape(bs, ss, d)
    w_keep = min(WINDOW, sp)
    kv4 = lambda t, b, s: t.reshape(b, s, N_KV_HEADS, HEAD_DIM)
    win_k_prompt = kv4(kp, bp, sp)[:, None, sp - w_keep:]
    win_v_prompt = kv4(vp, bp, sp)[:, None, sp - w_keep:]
    win_k_sample = jnp.concatenate([state_win_k[:, 0], kv4(ks_new, bs, 1)], axis=1)[:, None, ss:]
    win_v_sample = jnp.concatenate([state_win_v[:, 0], kv4(vs_new, bs, 1)], axis=1)[:, None, ss:]
    return (y_prompt, y_sample, win_k_prompt, win_v_prompt,
            kv4(kv_kp, bp, sp), kv4(kv_vp, bp, sp),
            win_k_sample, win_v_sample,
            kv4(kv_ks, bs, 1), kv4(kv_vs, bs, 1))
```

```python
import functools
import math

import numpy as np
import jax
import jax.numpy as jnp
from jax import lax
from jax.experimental import pallas as pl
from jax.experimental.pallas import tpu as pltpu

N_HEADS = 16
N_KV_HEADS = 4
HEAD_DIM = 64
GROUP = N_HEADS // N_KV_HEADS
WINDOW = 128
BAND_BLOCK = 128
MOBA_BLOCK = 256
MOBA_TOPK = 3
MOBA_Q_BLOCK = 128
PAGE_SIZE = 128
N_BUCKETS = 32
REL_MAX_DIST = 1024
N_EXPERTS = 8
EXPERT_TOPK = 2
RMS_EPS = 1e-5
NEG_INF = -1e30
ATTN_SCALE = HEAD_DIM ** -0.5
LOG2E = math.log2(math.e)
QCOLS = GROUP * MOBA_Q_BLOCK
FAR_DIST = 790
MOBA_FAR_TILE = 9
SAMPLE_FAR_TILE = 4
SUBLANES = 8
LANES = 128

F32 = jnp.float32
BF16 = jnp.bfloat16
VMEM_LIMIT = 56 * 1024 * 1024


def _bucket_np(dist):
    max_exact = N_BUCKETS // 2
    n = np.maximum(np.asarray(dist, np.int64), 0)
    nf = np.maximum(n, 1).astype(np.float64)
    large = max_exact + (np.log(nf / max_exact) / math.log(REL_MAX_DIST / max_exact)
                         * (N_BUCKETS - max_exact)).astype(np.int64)
    return np.where(n < max_exact, n, np.minimum(large, N_BUCKETS - 1)).astype(np.int32)


def _tile(n, pref):
    if n <= pref:
        return n
    for t in range(pref, 7, -8):
        if n % t == 0:
            return t
    return n


def _params(**kw):
    return pltpu.CompilerParams(vmem_limit_bytes=VMEM_LIMIT, **kw)


def _rms(x, g):
    return (x * lax.rsqrt(jnp.mean(x * x, axis=-1, keepdims=True) + RMS_EPS)) * g


def _rms_matmul_kernel(x_ref, g_ref, w_ref, *o_refs):
    xn = _rms(x_ref[...], g_ref[...]).astype(BF16)
    z = jnp.dot(xn, w_ref[...], preferred_element_type=F32)
    col = 0
    for o_ref in o_refs:
        o_ref[...] = z[:, col:col + o_ref.shape[1]]
        col += o_ref.shape[1]


def rms_matmul(x, g, w, tm, widths=None):
    m, d = x.shape
    n = w.shape[1]
    widths = widths or (n,)
    assert sum(widths) == n
    outs = pl.pallas_call(
        _rms_matmul_kernel,
        grid=(m // tm,),
        in_specs=[pl.BlockSpec((tm, d), lambda i: (i, 0)),
                  pl.BlockSpec((1, d), lambda i: (0, 0)),
                  pl.BlockSpec((d, n), lambda i: (0, 0))],
        out_specs=[pl.BlockSpec((tm, wd), lambda i: (i, 0)) for wd in widths],
        out_shape=[jax.ShapeDtypeStruct((m, wd), F32) for wd in widths],
        compiler_params=_params(),
        name="rms_matmul",
    )(x, g.reshape(1, d), w)
    return outs if len(widths) > 1 else outs[0]


def _matmul_res_kernel(a_ref, w_ref, r_ref, o_ref):
    o_ref[...] = r_ref[...] + jnp.dot(a_ref[...].astype(BF16), w_ref[...], preferred_element_type=F32)


def matmul_res(a, w, res, tm):
    m, k = a.shape
    n = w.shape[1]
    return pl.pallas_call(
        _matmul_res_kernel,
        grid=(m // tm,),
        in_specs=[pl.BlockSpec((tm, k), lambda i: (i, 0)),
                  pl.BlockSpec((k, n), lambda i: (0, 0)),
                  pl.BlockSpec((tm, n), lambda i: (i, 0))],
        out_specs=pl.BlockSpec((tm, n), lambda i: (i, 0)),
        out_shape=jax.ShapeDtypeStruct((m, n), F32),
        compiler_params=_params(),
        name="matmul_res",
    )(a, w, res)


def _ffn_kernel(x_ref, g_ref, wg_ref, wu_ref, wd_ref, o_ref, xn_ref, acc_ref):
    j = pl.program_id(1)

    @pl.when(j == 0)
    def _():
        xn_ref[...] = _rms(x_ref[...], g_ref[...]).astype(BF16)
        acc_ref[...] = jnp.zeros_like(acc_ref)

    xn = xn_ref[...]
    a = jnp.dot(xn, wg_ref[...], preferred_element_type=F32)
    b = jnp.dot(xn, wu_ref[...], preferred_element_type=F32)
    h = (a * jax.nn.sigmoid(a)) * b
    acc_ref[...] += jnp.dot(h.astype(BF16), wd_ref[...], preferred_element_type=F32)

    @pl.when(j == pl.num_programs(1) - 1)
    def _():
        o_ref[...] = x_ref[...] + acc_ref[...]


def ffn_dense(x, g, wg, wu, wd, tm, tn):
    m, d = x.shape
    f = wg.shape[1]
    return pl.pallas_call(
        _ffn_kernel,
        grid=(m // tm, f // tn),
        in_specs=[pl.BlockSpec((tm, d), lambda i, j: (i, 0)),
                  pl.BlockSpec((1, d), lambda i, j: (0, 0)),
                  pl.BlockSpec((d, tn), lambda i, j: (0, j)),
                  pl.BlockSpec((d, tn), lambda i, j: (0, j)),
                  pl.BlockSpec((tn, d), lambda i, j: (j, 0))],
        out_specs=pl.BlockSpec((tm, d), lambda i, j: (i, 0)),
        out_shape=jax.ShapeDtypeStruct((m, d), F32),
        scratch_shapes=[pltpu.VMEM((tm, d), BF16), pltpu.VMEM((tm, d), F32)],
        compiler_params=_params(),
        name="ffn_dense",
    )(x, g.reshape(1, d), wg, wu, wd)


def _dot_tn(a, b):
    return lax.dot_general(a, b, (((0,), (0,)), ((), ())), preferred_element_type=F32)


def _heads_to_cols(q_rows):
    qt = q_rows.T
    return jnp.concatenate([qt[r * HEAD_DIM:(r + 1) * HEAD_DIM] for r in range(GROUP)], axis=1)


def _cols_to_heads(o, n_q):
    ot = jnp.concatenate([o[:, r * n_q:(r + 1) * n_q] for r in range(GROUP)], axis=0)
    return ot.T


def _win_prompt_kernel(q_ref, kp_ref, ko_ref, vp_ref, vo_ref, wb_ref, sink_ref, o_ref):
    n = pl.program_id(1)
    row = lax.broadcasted_iota(jnp.int32, (2 * BAND_BLOCK, QCOLS), 0)
    has_prev = (row >= BAND_BLOCK) | (n > 0)
    gw = GROUP * HEAD_DIM
    for g in range(N_KV_HEADS):
        q = (_heads_to_cols(q_ref[:, g * gw:(g + 1) * gw]) * ATTN_SCALE).astype(BF16)
        kt = jnp.concatenate([kp_ref[0, g], ko_ref[0, g]], axis=1).astype(BF16)
        s = _dot_tn(kt, q) + wb_ref[g]
        s = jnp.where(has_prev, s, NEG_INF)
        sink = sink_ref[g]
        m = jnp.maximum(jnp.max(s, axis=0, keepdims=True), sink)
        e = jnp.exp(s - m)
        den = jnp.sum(e, axis=0, keepdims=True) + jnp.exp(sink - m)
        vt = jnp.concatenate([vp_ref[0, g], vo_ref[0, g]], axis=1).astype(BF16)
        o = jnp.dot(vt, e.astype(BF16), preferred_element_type=F32) / den
        o_ref[:, g * gw:(g + 1) * gw] = _cols_to_heads(o, BAND_BLOCK).astype(o_ref.dtype)


def window_attn_prompt(q, kt, vt, wb, sink_cols):
    b, _, _, s = kt.shape
    hq = q.shape[1]
    nb = s // BAND_BLOCK
    qb = BAND_BLOCK
    prev = lambda n: jnp.maximum(n - 1, 0)
    return pl.pallas_call(
        _win_prompt_kernel,
        grid=(b, nb),
        in_specs=[pl.BlockSpec((qb, hq), lambda b, n: (b * nb + n, 0)),
                  pl.BlockSpec((1, N_KV_HEADS, HEAD_DIM, qb), lambda b, n: (b, 0, 0, prev(n))),
                  pl.BlockSpec((1, N_KV_HEADS, HEAD_DIM, qb), lambda b, n: (b, 0, 0, n)),
                  pl.BlockSpec((1, N_KV_HEADS, HEAD_DIM, qb), lambda b, n: (b, 0, 0, prev(n))),
                  pl.BlockSpec((1, N_KV_HEADS, HEAD_DIM, qb), lambda b, n: (b, 0, 0, n)),
                  pl.BlockSpec((N_KV_HEADS, 2 * qb, QCOLS), lambda b, n: (0, 0, 0)),
                  pl.BlockSpec((N_KV_HEADS, 1, QCOLS), lambda b, n: (0, 0, 0))],
        out_specs=pl.BlockSpec((qb, hq), lambda b, n: (b * nb + n, 0)),
        out_shape=jax.ShapeDtypeStruct(q.shape, BF16),
        compiler_params=_params(),
        name="window_attn_prompt",
    )(q, kt, kt, vt, vt, wb, sink_cols)


def _moba_prompt_kernel(q_ref, k_ref, vt_ref, tb_ref, o_ref, kmean_ref, sel_ref, *, n_blk):
    qb = pl.program_id(2)
    tiles_per_blk = MOBA_BLOCK // MOBA_Q_BLOCK

    @pl.when(qb == 0)
    def _():
        lane = lax.broadcasted_iota(jnp.int32, kmean_ref.shape, 1)
        means = jnp.zeros(kmean_ref.shape, F32)
        for j in range(n_blk):
            col = jnp.mean(k_ref[0, 0, :, j * MOBA_BLOCK:(j + 1) * MOBA_BLOCK], axis=1, keepdims=True)
            means = jnp.where(lane == j, col, means)
        kmean_ref[...] = means

    kmean_t = kmean_ref[...].astype(BF16)
    q_all = _heads_to_cols(q_ref[...])
    qs = []
    for tt in range(tiles_per_blk):
        q32 = jnp.concatenate([q_all[:, r * MOBA_BLOCK + tt * MOBA_Q_BLOCK:
                                     r * MOBA_BLOCK + (tt + 1) * MOBA_Q_BLOCK]
                               for r in range(GROUP)], axis=1)
        gate = _dot_tn(kmean_t, q32.astype(BF16))[:n_blk]
        blk = lax.broadcasted_iota(jnp.int32, gate.shape, 0)
        past = blk < qb
        gate = jnp.where(past, gate, NEG_INF)
        rank = jnp.zeros(gate.shape, jnp.int32)
        for kk in range(n_blk):
            gk = gate[kk:kk + 1, :]
            beats = (gk > gate) | ((gk == gate) & (kk < blk))
            rank = rank + beats.astype(jnp.int32)
        sel = (past & (rank < MOBA_TOPK)) | (blk == qb)
        sel_ref[tt] = sel.astype(F32)
        qs.append((q32 * (ATTN_SCALE * LOG2E)).astype(BF16))

    def masked_logits(tt, s, j):
        u = jnp.clip(qb * tiles_per_blk + tt - tiles_per_blk * j, 0, MOBA_FAR_TILE)
        return jnp.where(sel_ref[tt, pl.ds(j, 1), :] > 0, s + tb_ref[0, u], NEG_INF)

    def body(jp, carry):
        off = pl.multiple_of(jp * (2 * MOBA_BLOCK), 2 * MOBA_BLOCK)
        kj = k_ref[0, 0, :, pl.ds(off, 2 * MOBA_BLOCK)].astype(BF16)
        vj = vt_ref[0, 0, :, pl.ds(off, 2 * MOBA_BLOCK)].astype(BF16)
        out = []
        for tt in range(tiles_per_blk):
            m, l, acc = carry[3 * tt:3 * tt + 3]
            s = _dot_tn(kj, qs[tt])
            s0 = masked_logits(tt, s[:MOBA_BLOCK], 2 * jp)
            s1 = masked_logits(tt, s[MOBA_BLOCK:], 2 * jp + 1)
            m_new = jnp.maximum(m, jnp.maximum(jnp.max(s0, axis=0, keepdims=True),
                                               jnp.max(s1, axis=0, keepdims=True)))
            alpha = jnp.exp2(m - m_new)
            p0 = jnp.exp2(s0 - m_new)
            p1 = jnp.exp2(s1 - m_new)
            l = alpha * l + (jnp.sum(p0, axis=0, keepdims=True) + jnp.sum(p1, axis=0, keepdims=True))
            acc = (alpha * acc
                   + jnp.dot(vj[:, :MOBA_BLOCK], p0.astype(BF16), preferred_element_type=F32)
                   + jnp.dot(vj[:, MOBA_BLOCK:], p1.astype(BF16), preferred_element_type=F32))
            out += [m_new, l, acc]
        return tuple(out)

    init = (jnp.full((1, QCOLS), NEG_INF, F32), jnp.zeros((1, QCOLS), F32),
            jnp.zeros((HEAD_DIM, QCOLS), F32)) * tiles_per_blk
    final = lax.fori_loop(0, qb // 2 + 1, body, init)
    outs = [final[3 * tt + 2] / final[3 * tt + 1] for tt in range(tiles_per_blk)]
    o_all = jnp.concatenate([o[:, r * MOBA_Q_BLOCK:(r + 1) * MOBA_Q_BLOCK]
                             for r in range(GROUP) for o in outs], axis=1)
    o_ref[...] = _cols_to_heads(o_all, MOBA_BLOCK).astype(o_ref.dtype)


def moba_attn_prompt(q, kt, vt, tb):
    b, _, _, s = kt.shape
    gw = GROUP * HEAD_DIM
    n_blk = s // MOBA_BLOCK
    assert n_blk % 2 == 0, "kv blocks are swept in pairs"
    assert n_blk <= LANES
    n_tb = tb.shape[1]
    tiles = MOBA_BLOCK // MOBA_Q_BLOCK
    return pl.pallas_call(
        functools.partial(_moba_prompt_kernel, n_blk=n_blk),
        grid=(N_KV_HEADS, b, n_blk),
        in_specs=[pl.BlockSpec((MOBA_BLOCK, gw), lambda g, b, t: (b * n_blk + t, g)),
                  pl.BlockSpec((1, 1, HEAD_DIM, s), lambda g, b, t: (b, g, 0, 0)),
                  pl.BlockSpec((1, 1, HEAD_DIM, s), lambda g, b, t: (b, g, 0, 0)),
                  pl.BlockSpec((1, n_tb, MOBA_BLOCK, QCOLS), lambda g, b, t: (g, 0, 0, 0))],
        out_specs=pl.BlockSpec((MOBA_BLOCK, gw), lambda g, b, t: (b * n_blk + t, g)),
        out_shape=jax.ShapeDtypeStruct(q.shape, BF16),
        scratch_shapes=[pltpu.VMEM((HEAD_DIM, LANES), F32),
                        pltpu.VMEM((tiles, n_blk, QCOLS), F32)],
        compiler_params=_params(),
        name="moba_attn_prompt",
    )(q, kt, vt, tb)


def _expand_heads(q, mask):
    return jnp.concatenate([q] * N_KV_HEADS, axis=1) * mask


def _fold_heads(o):
    acc = o
    for c in range(1, N_KV_HEADS):
        acc = acc + pltpu.roll(o, c * HEAD_DIM, axis=1)
    return acc[:, :HEAD_DIM]


def _win_sample_kernel(q_ref, kn_ref, vn_ref, knc_ref, vnc_ref, kb_ref, vb_ref, bias_ref, bias0_ref, sink_ref,
                       mask_ref, o_ref, ko_ref, vo_ref, *, bb):
    mask = mask_ref[...]
    sink = sink_ref[...]
    w = kb_ref.shape[2]
    last = lax.broadcasted_iota(jnp.int32, (1, w), 1) == w - 1
    rnd = lambda x: x.astype(BF16).astype(F32)
    for i in range(bb):
        kt = kb_ref[i]
        vt = vb_ref[i]
        qe = _expand_heads(q_ref[i] * ATTN_SCALE, mask).astype(BF16)
        s = jnp.dot(qe, kt.astype(BF16), preferred_element_type=F32) + bias_ref[...]
        s0 = jnp.sum(qe.astype(F32) * rnd(kn_ref[i]), axis=1, keepdims=True) + bias0_ref[...]
        m = jnp.maximum(jnp.maximum(jnp.max(s, axis=1, keepdims=True), s0), sink)
        e = jnp.exp(s - m)
        e0 = jnp.exp(s0 - m)
        inv = 1.0 / (jnp.sum(e, axis=1, keepdims=True) + e0 + jnp.exp(sink - m))
        o = (lax.dot_general((e * inv).astype(BF16), vt.astype(BF16), (((1,), (1,)), ((), ())),
                             preferred_element_type=F32)
             + rnd(e0 * inv) * rnd(vn_ref[i]))
        o_ref[i] = _fold_heads(o * mask)
        ko_ref[i] = jnp.where(last, knc_ref[i], pltpu.roll(kt, w - 1, axis=1))
        vo_ref[i] = jnp.where(last, vnc_ref[i], pltpu.roll(vt, w - 1, axis=1))


def window_attn_sample(q3, k_new, v_new, kt_buf, vt_buf, bias, bias0, sink, mask):
    db, kvw, w = kt_buf.shape
    bb = _tile(db, 8)
    row = lambda i: (i, 0, 0)
    fixed = lambda i: (0, 0)
    buf_spec = pl.BlockSpec((bb, kvw, w), row)
    return pl.pallas_call(
        functools.partial(_win_sample_kernel, bb=bb),
        grid=(db // bb,),
        in_specs=[pl.BlockSpec((bb, N_HEADS, HEAD_DIM), row),
                  pl.BlockSpec((bb, 1, kvw), row),
                  pl.BlockSpec((bb, 1, kvw), row),
                  pl.BlockSpec((bb, kvw, 1), row),
                  pl.BlockSpec((bb, kvw, 1), row),
                  buf_spec, buf_spec,
                  pl.BlockSpec((N_HEADS, w), fixed),
                  pl.BlockSpec((N_HEADS, 1), fixed),
                  pl.BlockSpec((N_HEADS, 1), fixed),
                  pl.BlockSpec((N_HEADS, kvw), fixed)],
        out_specs=[pl.BlockSpec((bb, N_HEADS, HEAD_DIM), row), buf_spec, buf_spec],
        out_shape=[jax.ShapeDtypeStruct((db, N_HEADS, HEAD_DIM), F32),
                   jax.ShapeDtypeStruct(kt_buf.shape, F32),
                   jax.ShapeDtypeStruct(vt_buf.shape, F32)],
        compiler_params=_params(),
        name="window_attn_sample",
    )(q3, k_new[:, None, :], v_new[:, None, :], k_new[:, :, None], v_new[:, :, None],
      kt_buf, vt_buf, bias, bias0, sink, mask)


def _copy_page(cache_ref, page, dst_ref, sem):
    return pltpu.make_async_copy(cache_ref.at[page], dst_ref, sem)


def _page_gate_kernel(pt_ref, cache_ref, qm_ref, o_ref, buf_ref, sem_ref, *, cpp, chunks_per_b):
    step = pl.program_id(0)
    n_steps = pl.num_programs(0)

    def copies(s, slot):
        b = s // chunks_per_b
        c = s % chunks_per_b
        return [_copy_page(cache_ref, pt_ref[b, c * cpp + p], buf_ref.at[slot, p], sem_ref.at[slot])
                for p in range(cpp)]

    @pl.when(step == 0)
    def _():
        for cp in copies(step, 0):
            cp.start()

    @pl.when(step + 1 < n_steps)
    def _():
        for cp in copies(step + 1, (step + 1) % 2):
            cp.start()

    slot = step % 2
    for cp in copies(step, slot):
        cp.wait()
    ppb = MOBA_BLOCK // PAGE_SIZE
    qm = qm_ref[0].astype(BF16).astype(F32)
    for n in range(cpp // ppb):
        x = jnp.sum(buf_ref[slot, pl.ds(n * ppb, ppb)], axis=0)
        mean = jnp.sum(x, axis=-1, keepdims=True) * (1.0 / MOBA_BLOCK)
        mean = mean.astype(BF16).astype(F32)
        o_ref[0, n:n + 1, :] = jnp.sum(jnp.sum(mean * qm, axis=0), axis=0, keepdims=True)


def page_block_gates(page_table, cache_t, qm, n_full):
    db = page_table.shape[0]
    ppb = MOBA_BLOCK // PAGE_SIZE
    n_pages = n_full * ppb
    cpp = _tile(n_pages, 64)
    chunks_per_b = n_pages // cpp
    return pl.pallas_call(
        functools.partial(_page_gate_kernel, cpp=cpp, chunks_per_b=chunks_per_b),
        grid_spec=pltpu.PrefetchScalarGridSpec(
            num_scalar_prefetch=1,
            grid=(db * chunks_per_b,),
            in_specs=[pl.BlockSpec(memory_space=pl.ANY),
                      pl.BlockSpec((1, N_KV_HEADS, HEAD_DIM, N_HEADS),
                                   lambda s, pt: (s // chunks_per_b, 0, 0, 0))],
            out_specs=pl.BlockSpec((1, cpp // ppb, N_HEADS),
                                   lambda s, pt: (s // chunks_per_b, s % chunks_per_b, 0)),
            scratch_shapes=[pltpu.VMEM((2, cpp, N_KV_HEADS, HEAD_DIM, PAGE_SIZE), F32),
                            pltpu.SemaphoreType.DMA((2,))]),
        out_shape=jax.ShapeDtypeStruct((db, n_full, N_HEADS), F32),
        compiler_params=_params(dimension_semantics=("arbitrary",)),
        name="page_block_gates",
    )(page_table, cache_t, qm)


def _moba_select_kernel(gate_ref, sel_ref, *, bb, n_sel):
    for i in range(bb):
        gate = gate_ref[i]
        n_full = gate.shape[0]
        row = lax.broadcasted_iota(jnp.int32, gate.shape, 0).astype(F32)
        picks = []
        for _ in range(n_sel):
            best = jnp.max(gate, axis=0, keepdims=True)
            idx = jnp.min(jnp.where(gate == best, row, float(n_full)), axis=0, keepdims=True)
            picks.append(idx)
            gate = jnp.where(row == idx, -jnp.inf, gate)
        sel_ref[i] = jnp.concatenate(picks, axis=0).astype(jnp.int32)


def moba_select_sample(gates, n_sel):
    db, n_full, _ = gates.shape
    bb = _tile(db, 8)
    return pl.pallas_call(
        functools.partial(_moba_select_kernel, bb=bb, n_sel=n_sel),
        grid=(db // bb,),
        in_specs=[pl.BlockSpec((bb, n_full, N_HEADS), lambda i: (i, 0, 0))],
        out_specs=pl.BlockSpec((bb, n_sel, N_HEADS), lambda i: (i, 0, 0)),
        out_shape=jax.ShapeDtypeStruct((db, n_sel, N_HEADS), jnp.int32),
        compiler_params=_params(),
        name="moba_select_sample",
    )(gates)


def _moba_sample_kernel(phys_ref, tile_ref, ck_ref, cv_ref, q_ref, kn_ref, vn_ref, bt_ref, bias0_ref,
                        o_ref, kbuf_ref, vbuf_ref, s_ref, p_ref, sem_ref, *, n_sel):
    b = pl.program_id(0)
    n_b = pl.num_programs(0)
    ppb = MOBA_BLOCK // PAGE_SIZE
    n_pg = n_sel * ppb

    def copies(bi, slot):
        out = []
        for h in range(N_HEADS):
            g = h // GROUP
            for p in range(n_pg):
                page = phys_ref[bi, h * n_pg + p]
                out.append(pltpu.make_async_copy(ck_ref.at[page, g], kbuf_ref.at[slot, h, p],
                                                 sem_ref.at[0, slot]))
                out.append(pltpu.make_async_copy(cv_ref.at[page, g], vbuf_ref.at[slot, h, p],
                                                 sem_ref.at[1, slot]))
        return out

    @pl.when(b == 0)
    def _():
        for cp in copies(b, 0):
            cp.start()

    @pl.when(b + 1 < n_b)
    def _():
        for cp in copies(b + 1, (b + 1) % 2):
            cp.start()

    slot = b % 2
    for cp in copies(b, slot):
        cp.wait()

    rnd = lambda x: x.astype(BF16).astype(F32)
    n_rows = s_ref.shape[1]
    lane = lax.broadcasted_iota(jnp.int32, (1, PAGE_SIZE), 1)
    if n_rows > n_pg + 1:
        s_ref[:, n_pg + 1:, :] = jnp.full((N_HEADS, n_rows - n_pg - 1, PAGE_SIZE), NEG_INF, F32)
    for h in range(N_HEADS):
        g = h // GROUP
        q = rnd(q_ref[0, :, h:h + 1] * ATTN_SCALE)
        for si in range(n_sel):
            tile = tile_ref[b, h * n_sel + si]
            for pp in range(ppb):
                p = si * ppb + pp
                s_ref[h, p:p + 1, :] = (jnp.sum(rnd(kbuf_ref[slot, h, p]) * q, axis=0, keepdims=True)
                                        + bt_ref[tile, h, pp:pp + 1, :])
        s0 = (jnp.sum(q * rnd(kn_ref[0, :, g:g + 1]), axis=0, keepdims=True)
              + bias0_ref[:, h:h + 1])
        s_ref[h, n_pg:n_pg + 1, :] = jnp.where(lane == 0, s0, NEG_INF)
    s = s_ref[...]
    m = jnp.max(jnp.max(s, axis=2, keepdims=True), axis=1, keepdims=True)
    e = jnp.exp(s - m)
    den = jnp.sum(jnp.sum(e, axis=2, keepdims=True), axis=1, keepdims=True)
    p_ref[...] = rnd(e * (1.0 / den))
    head = lax.broadcasted_iota(jnp.int32, (1, N_HEADS), 1)
    o_all = jnp.zeros((HEAD_DIM, N_HEADS), F32)
    for h in range(N_HEADS):
        g = h // GROUP
        acc = rnd(vbuf_ref[slot, h, 0]) * p_ref[h, 0:1, :]
        for pg in range(1, n_pg):
            acc = acc + rnd(vbuf_ref[slot, h, pg]) * p_ref[h, pg:pg + 1, :]
        o = (jnp.sum(acc, axis=1, keepdims=True)
             + p_ref[h, n_pg:n_pg + 1, 0:1] * rnd(vn_ref[0, :, g:g + 1]))
        o_all = jnp.where(head == h, o, o_all)
    o_ref[0] = o_all


def moba_attn_sample(phys, tiles, cache_kt, cache_vt, q_cols, k_new, v_new, bias_tiles, bias0, n_sel):
    db = q_cols.shape[0]
    n_pg = n_sel * (MOBA_BLOCK // PAGE_SIZE)
    n_rows = -(-(n_pg + 1) // SUBLANES) * SUBLANES
    row = lambda b, *_: (b, 0, 0)
    return pl.pallas_call(
        functools.partial(_moba_sample_kernel, n_sel=n_sel),
        grid_spec=pltpu.PrefetchScalarGridSpec(
            num_scalar_prefetch=2,
            grid=(db,),
            in_specs=[pl.BlockSpec(memory_space=pl.ANY),
                      pl.BlockSpec(memory_space=pl.ANY),
                      pl.BlockSpec((1, HEAD_DIM, N_HEADS), row),
                      pl.BlockSpec((1, HEAD_DIM, N_KV_HEADS), row),
                      pl.BlockSpec((1, HEAD_DIM, N_KV_HEADS), row),
                      pl.BlockSpec(bias_tiles.shape, lambda b, *_: (0, 0, 0, 0)),
                      pl.BlockSpec((1, N_HEADS), lambda b, *_: (0, 0))],
            out_specs=pl.BlockSpec((1, HEAD_DIM, N_HEADS), row),
            scratch_shapes=[pltpu.VMEM((2, N_HEADS, n_pg, HEAD_DIM, PAGE_SIZE), F32),
                            pltpu.VMEM((2, N_HEADS, n_pg, HEAD_DIM, PAGE_SIZE), F32),
                            pltpu.VMEM((N_HEADS, n_rows, PAGE_SIZE), F32),
                            pltpu.VMEM((N_HEADS, n_rows, PAGE_SIZE), F32),
                            pltpu.SemaphoreType.DMA((2, 2))]),
        out_shape=jax.ShapeDtypeStruct((db, HEAD_DIM, N_HEADS), F32),
        compiler_params=_params(dimension_semantics=("arbitrary",)),
        name="moba_attn_sample",
    )(phys, tiles, cache_kt, cache_vt, q_cols, k_new, v_new, bias_tiles, bias0)


def _router_kernel(x_ref, g_ref, wr_ref, xn_ref, gate_ref, idx_ref):
    xn = _rms(x_ref[...], g_ref[...])
    xn_ref[...] = xn
    logits = jnp.dot(xn.astype(BF16), wr_ref[...], preferred_element_type=F32)
    n_e = logits.shape[1]
    col = lax.broadcasted_iota(jnp.int32, logits.shape, 1).astype(F32)
    m1 = jnp.max(logits, axis=1, keepdims=True)
    i1 = jnp.min(jnp.where(logits == m1, col, float(n_e)), axis=1, keepdims=True)
    rest = jnp.where(col == i1, -jnp.inf, logits)
    m2 = jnp.max(rest, axis=1, keepdims=True)
    i2 = jnp.min(jnp.where(rest == m2, col, float(n_e)), axis=1, keepdims=True)
    e2 = jnp.exp(m2 - m1)
    den = 1.0 + e2
    gate_ref[...] = jnp.concatenate([1.0 / den, e2 / den], axis=1)
    idx_ref[...] = jnp.concatenate([i1, i2], axis=1).astype(jnp.int32)


def _router_into_kernel(x_ref, g_ref, wr_ref, prev_ref, xn_ref, gate_ref, idx_ref):
    del prev_ref
    _router_kernel(x_ref, g_ref, wr_ref, xn_ref, gate_ref, idx_ref)


def moe_router(x, g, w_router, tm, xn_all, row0):
    m, d = x.shape
    n_e = w_router.shape[1]
    assert row0 % tm == 0
    first = row0 // tm
    return pl.pallas_call(
        _router_into_kernel,
        grid=(m // tm,),
        in_specs=[pl.BlockSpec((tm, d), lambda i: (i, 0)),
                  pl.BlockSpec((1, d), lambda i: (0, 0)),
                  pl.BlockSpec((d, n_e), lambda i: (0, 0)),
                  pl.BlockSpec(memory_space=pl.ANY)],
        out_specs=[pl.BlockSpec((tm, d), lambda i: (first + i, 0)),
                   pl.BlockSpec((tm, EXPERT_TOPK), lambda i: (i, 0)),
                   pl.BlockSpec((tm, EXPERT_TOPK), lambda i: (i, 0))],
        out_shape=[jax.ShapeDtypeStruct(xn_all.shape, F32),
                   jax.ShapeDtypeStruct((m, EXPERT_TOPK), F32),
                   jax.ShapeDtypeStruct((m, EXPERT_TOPK), jnp.int32)],
        input_output_aliases={3: 0},
        compiler_params=_params(),
        name="moe_router",
    )(x, g.reshape(1, d), w_router, xn_all)


def _moe_ffn_kernel(te_ref, live_ref, x_ref, wg_ref, wu_ref, wd_ref, prev_ref, o_ref, xb_ref, acc_ref):
    del prev_ref
    i = pl.program_id(0)
    j = pl.program_id(1)

    @pl.when(j == 0)
    def _():
        xb_ref[...] = x_ref[...].astype(BF16)
        acc_ref[...] = jnp.zeros_like(acc_ref)

    @pl.when(live_ref[i] > 0)
    def _():
        x = xb_ref[...]
        a = jnp.dot(x, wg_ref[0].astype(BF16), preferred_element_type=F32)
        b = jnp.dot(x, wu_ref[0].astype(BF16), preferred_element_type=F32)
        h = (a * jax.nn.sigmoid(a)) * b
        acc_ref[...] += jnp.dot(h.astype(BF16), wd_ref[0].astype(BF16), preferred_element_type=F32)

    @pl.when(j == pl.num_programs(1) - 1)
    def _():
        o_ref[...] = acc_ref[...]


def moe_ffn(tile_expert, tile_live, x_sorted, wg, wu, wd, e_sorted, first_tile, tm, tn):
    n_tiles = tile_expert.shape[0]
    d = x_sorted.shape[1]
    f = wg.shape[2]
    return pl.pallas_call(
        _moe_ffn_kernel,
        grid_spec=pltpu.PrefetchScalarGridSpec(
            num_scalar_prefetch=2,
            grid=(n_tiles, f // tn),
            in_specs=[pl.BlockSpec((tm, d), lambda i, j, te, lv: (i, 0)),
                      pl.BlockSpec((1, d, tn), lambda i, j, te, lv: (te[i], 0, j * lv[i])),
                      pl.BlockSpec((1, d, tn), lambda i, j, te, lv: (te[i], 0, j * lv[i])),
                      pl.BlockSpec((1, tn, d), lambda i, j, te, lv: (te[i], j * lv[i], 0)),
                      pl.BlockSpec(memory_space=pl.ANY)],
            out_specs=pl.BlockSpec((tm, d), lambda i, j, te, lv: (first_tile + i, 0)),
            scratch_shapes=[pltpu.VMEM((tm, d), BF16), pltpu.VMEM((tm, d), F32)]),
        out_shape=jax.ShapeDtypeStruct(e_sorted.shape, F32),
        input_output_aliases={6: 0},
        compiler_params=_params(),
        name="moe_ffn",
    )(tile_expert, tile_live, x_sorted, wg, wu, wd, e_sorted)


def _combine_norm_kernel(y_ref, e1_ref, e2_ref, gate_ref, g_ref, o_ref):
    gate = gate_ref[...]
    y = y_ref[...] + (gate[:, 0:1] * e1_ref[...] + gate[:, 1:2] * e2_ref[...])
    o_ref[...] = _rms(y, g_ref[...])


def combine_norm(y, e1, e2, gates, g, tm):
    m, d = y.shape
    blk = pl.BlockSpec((tm, d), lambda i: (i, 0))
    return pl.pallas_call(
        _combine_norm_kernel,
        grid=(m // tm,),
        in_specs=[blk, blk, blk,
                  pl.BlockSpec((tm, EXPERT_TOPK), lambda i: (i, 0)),
                  pl.BlockSpec((1, d), lambda i: (0, 0))],
        out_specs=blk,
        out_shape=jax.ShapeDtypeStruct((m, d), F32),
        compiler_params=_params(),
        name="combine_norm",
    )(y, e1, e2, gates, g.reshape(1, d))


def moe_dispatch_plan(idx, tm, n_chunks):
    m, k = idx.shape
    n_asg = m * k
    p = -(-(n_asg + N_EXPERTS * (tm - 1)) // (tm * n_chunks)) * (tm * n_chunks)
    flat = idx.reshape(n_asg)
    onehot = (flat[:, None] == jnp.arange(N_EXPERTS, dtype=jnp.int32)[None, :]).astype(jnp.int32)
    csum = jnp.cumsum(onehot, axis=0)
    count = csum[-1]
    rank = jnp.take_along_axis(csum, flat[:, None], axis=1)[:, 0] - 1
    padded = -(-count // tm) * tm
    pad_end = jnp.cumsum(padded)
    pad_start = pad_end - padded
    slot = pad_start[flat] + rank
    order = jnp.sort(flat * n_asg + jnp.arange(n_asg, dtype=jnp.int32)) % n_asg
    start = jnp.cumsum(count) - count
    slots = jnp.arange(p, dtype=jnp.int32)
    slot_e = jnp.minimum(jnp.sum((pad_end[None, :] <= slots[:, None]).astype(jnp.int32), axis=1), N_EXPERTS - 1)
    r = slots - pad_start[slot_e]
    live = r < count[slot_e]
    src_asg = order[jnp.clip(start[slot_e] + r, 0, n_asg - 1)]
    row_src = jnp.where(live, src_asg // k, 0).astype(jnp.int32)
    tile_expert = slot_e[::tm]
    tile_live = live[::tm].astype(jnp.int32)
    return row_src, slot.reshape(m, k), tile_expert, tile_live


def _toeplitz_offsets(rows, cols):
    x = np.arange(rows + cols)
    return np.where(x < cols, x, x - (rows + cols))


def _toeplitz(src, rows, cols):
    period = rows + cols
    lead = src.shape[:-1]
    flat = jnp.tile(src, (1,) * len(lead) + (rows,))[..., :rows * (period - 1)]
    return flat.reshape(*lead, rows, period - 1)[..., :cols]


def _distance_bias(table, dist, valid):
    bias = jnp.where(valid[..., None], table[_bucket_np(dist)], NEG_INF)
    return jnp.swapaxes(bias, -1, -2)


def _window_prompt_bias(table):
    rows, cols = 2 * BAND_BLOCK, BAND_BLOCK
    dist = BAND_BLOCK + _toeplitz_offsets(rows, cols)
    bias = _toeplitz(_distance_bias(table, dist, (dist >= 0) & (dist <= WINDOW)), rows, cols)
    bias = bias.reshape(N_KV_HEADS, GROUP, rows, cols).transpose(0, 2, 1, 3)
    return bias.reshape(N_KV_HEADS, rows, QCOLS)


def _moba_prompt_bias(table):
    rows, cols = MOBA_BLOCK, MOBA_Q_BLOCK
    n_t = MOBA_FAR_TILE + 1
    u = np.arange(n_t)[:, None]
    dist = u * MOBA_Q_BLOCK + _toeplitz_offsets(rows, cols)[None, :]
    dist = np.where(u == MOBA_FAR_TILE, np.maximum(dist, FAR_DIST), dist)
    bias = _toeplitz(_distance_bias(table, dist, dist >= 0), rows, cols)
    bias = bias.reshape(n_t, N_KV_HEADS, GROUP, rows, cols).transpose(1, 0, 3, 2, 4)
    return bias.reshape(N_KV_HEADS, n_t, rows, QCOLS)


def _moba_sample_bias(table, past_len):
    ti = np.arange(SAMPLE_FAR_TILE + 1)[:, None]
    r = np.arange(MOBA_BLOCK)[None, :]
    dist = (ti + 1) * MOBA_BLOCK - r + (past_len % MOBA_BLOCK)
    dist = np.where(ti == SAMPLE_FAR_TILE, np.maximum(dist, FAR_DIST), dist)
    bias = table[_bucket_np(dist)].transpose(0, 2, 1)
    return bias.reshape(SAMPLE_FAR_TILE + 1, N_HEADS, MOBA_BLOCK // PAGE_SIZE, PAGE_SIZE)


def kernel(x_prompt, x_sample, state_win_k, state_win_v, cache_k, cache_v, page_table, rel_bias_table,
           attn_norm_g, ffn_norm_g, w_qkv_a, sink_a, w_o_a, kv_norm_g, w_kv_shared, w_q_b, w_o_b,
           w_gate_dense, w_up_dense, w_down_dense, w_router, w_gate_moe, w_up_moe, w_down_moe, final_norm_g):
    bp, sp, d = x_prompt.shape
    bs, ss, _ = x_sample.shape
    assert ss == 1 and attn_norm_g.shape[0] == 2
    mp = bp * sp
    hq = N_HEADS * HEAD_DIM
    hk = N_KV_HEADS * HEAD_DIM
    past_len = page_table.shape[1] * PAGE_SIZE
    assert past_len % MOBA_BLOCK == 0 and sp % MOBA_BLOCK == 0
    n_full = past_len // MOBA_BLOCK
    n_sel = min(MOBA_TOPK, n_full)
    w_buf = state_win_k.shape[2]
    table = rel_bias_table.astype(F32)

    tm_p = _tile(mp, 512)
    tm_s = _tile(bs, 512)
    tn = _tile(w_gate_dense.shape[2], 896) if w_gate_dense.shape[2] % 128 == 0 else w_gate_dense.shape[2]
    cast = lambda w: w.astype(BF16)

    yp = x_prompt.reshape(mp, d)
    ys = x_sample.reshape(bs, d)

    w_qkv = cast(w_qkv_a[0])
    qp, kp, vp = rms_matmul(yp, attn_norm_g[0], w_qkv, tm_p, (hq, hk, hk))
    qs_new, ks_new, vs_new = rms_matmul(ys, attn_norm_g[0], w_qkv, tm_s, (hq, hk, hk))

    pos_minor = lambda t: t.reshape(bp, sp, N_KV_HEADS, HEAD_DIM).transpose(0, 2, 3, 1)

    sink_cols = jnp.repeat(sink_a[0].astype(F32).reshape(N_KV_HEADS, 1, GROUP), BAND_BLOCK, axis=2)
    mix_p = window_attn_prompt(qp, pos_minor(kp), pos_minor(vp), _window_prompt_bias(table), sink_cols)

    head_mask = jnp.asarray((np.arange(N_HEADS)[:, None] // GROUP == np.arange(hk)[None, :] // HEAD_DIM)
                            .astype(np.float32))
    rows_minor = lambda t: t[:, 0].transpose(0, 2, 3, 1).reshape(bs, hk, w_buf)
    bias_ws = table[_bucket_np(w_buf - np.arange(w_buf))].T
    bias_0 = table[0][:, None]
    mix_s, win_kt, win_vt = window_attn_sample(
        qs_new.reshape(bs, N_HEADS, HEAD_DIM), ks_new, vs_new, rows_minor(state_win_k), rows_minor(state_win_v),
        bias_ws, bias_0, sink_a[0].astype(F32)[:, None], head_mask)

    w_o = cast(w_o_a[0])
    yp = matmul_res(mix_p, w_o, yp, tm_p)
    ys = matmul_res(mix_s.reshape(bs, hq), w_o, ys, tm_s)

    wg, wu, wd = cast(w_gate_dense[0]), cast(w_up_dense[0]), cast(w_down_dense[0])
    yp = ffn_dense(yp, ffn_norm_g[0], wg, wu, wd, tm_p, tn)
    ys = ffn_dense(ys, ffn_norm_g[0], wg, wu, wd, tm_s, tn)

    w_kv = cast(w_kv_shared)
    kv_kp, kv_vp = rms_matmul(yp, kv_norm_g, w_kv, tm_p, (hk, hk))
    kv_ks, kv_vs = rms_matmul(ys, kv_norm_g, w_kv, tm_s, (hk, hk))
    w_q = cast(w_q_b[0])
    q_p = rms_matmul(yp, attn_norm_g[1], w_q, tm_p)
    q_s = rms_matmul(ys, attn_norm_g[1], w_q, tm_s)

    mix_p = moba_attn_prompt(q_p, pos_minor(kv_kp), pos_minor(kv_vp), _moba_prompt_bias(table) * LOG2E)

    cache_kt = cache_k.transpose(0, 2, 3, 1)
    cache_vt = cache_v.transpose(0, 2, 3, 1)
    q_cols = q_s.reshape(bs, N_HEADS, HEAD_DIM).transpose(0, 2, 1)
    group_mask = jnp.asarray((np.arange(N_KV_HEADS)[:, None, None] == np.arange(N_HEADS)[None, None, :] // GROUP)
                             .astype(np.float32))
    gates_s = page_block_gates(page_table, cache_kt, q_cols[:, None] * group_mask, n_full)
    sel = moba_select_sample(gates_s, n_sel).transpose(0, 2, 1)
    ppb = MOBA_BLOCK // PAGE_SIZE
    pages = sel[..., None] * ppb + jnp.arange(ppb, dtype=jnp.int32)
    phys = jnp.take_along_axis(page_table, pages.reshape(bs, -1), axis=1)
    tiles = jnp.clip(n_full - 1 - sel, 0, SAMPLE_FAR_TILE).reshape(bs, -1)
    to_cols = lambda t: t.reshape(bs, N_KV_HEADS, HEAD_DIM).transpose(0, 2, 1)
    o_cols = moba_attn_sample(phys, tiles, cache_kt, cache_vt, q_cols, to_cols(kv_ks), to_cols(kv_vs),
                              _moba_sample_bias(table, past_len), table[0][None, :], n_sel)
    mix_s = o_cols.transpose(0, 2, 1)

    w_o = cast(w_o_b[0])
    yp = matmul_res(mix_p, w_o, yp, tm_p)
    ys = matmul_res(mix_s.reshape(bs, hq), w_o, ys, tm_s)

    w_r = cast(w_router[0])
    xn = jnp.zeros((mp + bs, d), F32)
    xn, gates_p, idx_p = moe_router(yp, ffn_norm_g[1], w_r, tm_p, xn, 0)
    xn, gates_s, idx_s = moe_router(ys, ffn_norm_g[1], w_r, tm_s, xn, mp)
    idx = jnp.concatenate([idx_p, idx_s], axis=0)
    tm_e = 768
    f_moe = w_gate_moe.shape[3]
    tn_e = _tile(f_moe, 512) if f_moe % 128 == 0 else f_moe
    n_chunks = 4
    row_src, slot, tile_expert, tile_live = moe_dispatch_plan(idx, tm_e, n_chunks)
    tiles_per_chunk = tile_expert.shape[0] // n_chunks
    e_sorted = jnp.zeros((row_src.shape[0], d), F32)
    for c in range(n_chunks):
        tiles = slice(c * tiles_per_chunk, (c + 1) * tiles_per_chunk)
        rows = slice(c * tiles_per_chunk * tm_e, (c + 1) * tiles_per_chunk * tm_e)
        x_chunk = jnp.take(xn, row_src[rows], axis=0, mode="clip")
        e_sorted = moe_ffn(tile_expert[tiles], tile_live[tiles], x_chunk, w_gate_moe[0], w_up_moe[0],
                           w_down_moe[0], e_sorted, c * tiles_per_chunk, tm_e, tn_e)
    def combine(y, gates, slots, tm):
        e1 = jnp.take(e_sorted, slots[:, 0], axis=0, mode="clip")
        e2 = jnp.take(e_sorted, slots[:, 1], axis=0, mode="clip")
        return combine_norm(y, e1, e2, gates, final_norm_g, tm)

    y_prompt = combine(yp, gates_p, slot[:mp], tm_p).reshape(bp, sp, d)
    y_sample = combine(ys, gates_s, slot[mp:], tm_s).reshape(bs, ss, d)
    w_keep = min(WINDOW, sp)
    kv4 = lambda t, b, s: t.reshape(b, s, N_KV_HEADS, HEAD_DIM)
    win_k_prompt = kv4(kp, bp, sp)[:, None, sp - w_keep:]
    win_v_prompt = kv4(vp, bp, sp)[:, None, sp - w_keep:]
    rows_major = lambda t: t.reshape(bs, N_KV_HEADS, HEAD_DIM, w_buf).transpose(0, 3, 1, 2)[:, None]
    win_k_sample = rows_major(win_kt)
    win_v_sample = rows_major(win_vt)
    return (y_prompt, y_sample, win_k_prompt, win_v_prompt,
            kv4(kv_kp, bp, sp), kv4(kv_vp, bp, sp),
            win_k_sample, win_v_sample,
            kv4(kv_ks, bs, 1), kv4(kv_vs, bs, 1))
```

```python
import functools
import math

import numpy as np
import jax
import jax.numpy as jnp
from jax import lax
from jax.experimental import pallas as pl
from jax.experimental.pallas import tpu as pltpu

N_HEADS = 16
N_KV_HEADS = 4
HEAD_DIM = 64
GROUP = N_HEADS // N_KV_HEADS
WINDOW = 128
BAND_BLOCK = 128
MOBA_BLOCK = 256
MOBA_TOPK = 3
MOBA_Q_BLOCK = 128
PAGE_SIZE = 128
N_BUCKETS = 32
REL_MAX_DIST = 1024
N_EXPERTS = 8
EXPERT_TOPK = 2
RMS_EPS = 1e-5
NEG_INF = -1e30
ATTN_SCALE = HEAD_DIM ** -0.5
LOG2E = math.log2(math.e)
QCOLS = GROUP * MOBA_Q_BLOCK
FAR_DIST = 790
MOBA_FAR_TILE = 9
SAMPLE_FAR_TILE = 4
SUBLANES = 8
LANES = 128

F32 = jnp.float32
BF16 = jnp.bfloat16
VMEM_LIMIT = 56 * 1024 * 1024


def _bucket_np(dist):
    max_exact = N_BUCKETS // 2
    n = np.maximum(np.asarray(dist, np.int64), 0)
    nf = np.maximum(n, 1).astype(np.float64)
    large = max_exact + (np.log(nf / max_exact) / math.log(REL_MAX_DIST / max_exact)
                         * (N_BUCKETS - max_exact)).astype(np.int64)
    return np.where(n < max_exact, n, np.minimum(large, N_BUCKETS - 1)).astype(np.int32)


def _tile(n, pref):
    if n <= pref:
        return n
    for t in range(pref, 7, -8):
        if n % t == 0:
            return t
    return n


def _params(**kw):
    return pltpu.CompilerParams(vmem_limit_bytes=VMEM_LIMIT, **kw)


def _rms(x, g):
    return (x * lax.rsqrt(jnp.mean(x * x, axis=-1, keepdims=True) + RMS_EPS)) * g


def _rms_matmul_kernel(x_ref, g_ref, w_ref, *o_refs):
    xn = _rms(x_ref[...], g_ref[...]).astype(BF16)
    z = jnp.dot(xn, w_ref[...], preferred_element_type=F32)
    col = 0
    for o_ref in o_refs:
        o_ref[...] = z[:, col:col + o_ref.shape[1]]
        col += o_ref.shape[1]


def rms_matmul(x, g, w, tm, widths=None):
    m, d = x.shape
    n = w.shape[1]
    widths = widths or (n,)
    assert sum(widths) == n
    outs = pl.pallas_call(
        _rms_matmul_kernel,
        grid=(m // tm,),
        in_specs=[pl.BlockSpec((tm, d), lambda i: (i, 0)),
                  pl.BlockSpec((1, d), lambda i: (0, 0)),
                  pl.BlockSpec((d, n), lambda i: (0, 0))],
        out_specs=[pl.BlockSpec((tm, wd), lambda i: (i, 0)) for wd in widths],
        out_shape=[jax.ShapeDtypeStruct((m, wd), F32) for wd in widths],
        compiler_params=_params(),
        name="rms_matmul",
    )(x, g.reshape(1, d), w)
    return outs if len(widths) > 1 else outs[0]


def _matmul_res_kernel(a_ref, w_ref, r_ref, o_ref):
    o_ref[...] = r_ref[...] + jnp.dot(a_ref[...].astype(BF16), w_ref[...], preferred_element_type=F32)


def matmul_res(a, w, res, tm):
    m, k = a.shape
    n = w.shape[1]
    return pl.pallas_call(
        _matmul_res_kernel,
        grid=(m // tm,),
        in_specs=[pl.BlockSpec((tm, k), lambda i: (i, 0)),
                  pl.BlockSpec((k, n), lambda i: (0, 0)),
                  pl.BlockSpec((tm, n), lambda i: (i, 0))],
        out_specs=pl.BlockSpec((tm, n), lambda i: (i, 0)),
        out_shape=jax.ShapeDtypeStruct((m, n), F32),
        compiler_params=_params(),
        name="matmul_res",
    )(a, w, res)


def _ffn_kernel(x_ref, g_ref, wg_ref, wu_ref, wd_ref, o_ref, xn_ref, acc_ref):
    j = pl.program_id(1)

    @pl.when(j == 0)
    def _():
        xn_ref[...] = _rms(x_ref[...], g_ref[...]).astype(BF16)
        acc_ref[...] = jnp.zeros_like(acc_ref)

    xn = xn_ref[...]
    a = jnp.dot(xn, wg_ref[...], preferred_element_type=F32)
    b = jnp.dot(xn, wu_ref[...], preferred_element_type=F32)
    h = (a * jax.nn.sigmoid(a)) * b
    acc_ref[...] += jnp.dot(h.astype(BF16), wd_ref[...], preferred_element_type=F32)

    @pl.when(j == pl.num_programs(1) - 1)
    def _():
        o_ref[...] = x_ref[...] + acc_ref[...]


def ffn_dense(x, g, wg, wu, wd, tm, tn):
    m, d = x.shape
    f = wg.shape[1]
    return pl.pallas_call(
        _ffn_kernel,
        grid=(m // tm, f // tn),
        in_specs=[pl.BlockSpec((tm, d), lambda i, j: (i, 0)),
                  pl.BlockSpec((1, d), lambda i, j: (0, 0)),
                  pl.BlockSpec((d, tn), lambda i, j: (0, j)),
                  pl.BlockSpec((d, tn), lambda i, j: (0, j)),
                  pl.BlockSpec((tn, d), lambda i, j: (j, 0))],
        out_specs=pl.BlockSpec((tm, d), lambda i, j: (i, 0)),
        out_shape=jax.ShapeDtypeStruct((m, d), F32),
        scratch_shapes=[pltpu.VMEM((tm, d), BF16), pltpu.VMEM((tm, d), F32)],
        compiler_params=_params(),
        name="ffn_dense",
    )(x, g.reshape(1, d), wg, wu, wd)


def _dot_tn(a, b):
    return lax.dot_general(a, b, (((0,), (0,)), ((), ())), preferred_element_type=F32)


def _heads_to_cols(q_rows):
    qt = q_rows.T
    return jnp.concatenate([qt[r * HEAD_DIM:(r + 1) * HEAD_DIM] for r in range(GROUP)], axis=1)


def _cols_to_heads(o, n_q):
    ot = jnp.concatenate([o[:, r * n_q:(r + 1) * n_q] for r in range(GROUP)], axis=0)
    return ot.T


def _win_prompt_kernel(q_ref, kp_ref, ko_ref, vp_ref, vo_ref, wb_ref, sink_ref, o_ref):
    n = pl.program_id(1)
    row = lax.broadcasted_iota(jnp.int32, (2 * BAND_BLOCK, QCOLS), 0)
    has_prev = (row >= BAND_BLOCK) | (n > 0)
    gw = GROUP * HEAD_DIM
    for g in range(N_KV_HEADS):
        q = (_heads_to_cols(q_ref[:, g * gw:(g + 1) * gw]) * ATTN_SCALE).astype(BF16)
        kt = jnp.concatenate([kp_ref[0, g], ko_ref[0, g]], axis=1).astype(BF16)
        s = _dot_tn(kt, q) + wb_ref[g]
        s = jnp.where(has_prev, s, NEG_INF)
        sink = sink_ref[g]
        m = jnp.maximum(jnp.max(s, axis=0, keepdims=True), sink)
        e = jnp.exp(s - m)
        den = jnp.sum(e, axis=0, keepdims=True) + jnp.exp(sink - m)
        vt = jnp.concatenate([vp_ref[0, g], vo_ref[0, g]], axis=1).astype(BF16)
        o = jnp.dot(vt, e.astype(BF16), preferred_element_type=F32) / den
        o_ref[:, g * gw:(g + 1) * gw] = _cols_to_heads(o, BAND_BLOCK).astype(o_ref.dtype)


def window_attn_prompt(q, kt, vt, wb, sink_cols):
    b, _, _, s = kt.shape
    hq = q.shape[1]
    nb = s // BAND_BLOCK
    qb = BAND_BLOCK
    prev = lambda n: jnp.maximum(n - 1, 0)
    return pl.pallas_call(
        _win_prompt_kernel,
        grid=(b, nb),
        in_specs=[pl.BlockSpec((qb, hq), lambda b, n: (b * nb + n, 0)),
                  pl.BlockSpec((1, N_KV_HEADS, HEAD_DIM, qb), lambda b, n: (b, 0, 0, prev(n))),
                  pl.BlockSpec((1, N_KV_HEADS, HEAD_DIM, qb), lambda b, n: (b, 0, 0, n)),
                  pl.BlockSpec((1, N_KV_HEADS, HEAD_DIM, qb), lambda b, n: (b, 0, 0, prev(n))),
                  pl.BlockSpec((1, N_KV_HEADS, HEAD_DIM, qb), lambda b, n: (b, 0, 0, n)),
                  pl.BlockSpec((N_KV_HEADS, 2 * qb, QCOLS), lambda b, n: (0, 0, 0)),
                  pl.BlockSpec((N_KV_HEADS, 1, QCOLS), lambda b, n: (0, 0, 0))],
        out_specs=pl.BlockSpec((qb, hq), lambda b, n: (b * nb + n, 0)),
        out_shape=jax.ShapeDtypeStruct(q.shape, BF16),
        compiler_params=_params(),
        name="window_attn_prompt",
    )(q, kt, kt, vt, vt, wb, sink_cols)


def _moba_prompt_kernel(q_ref, k_ref, vt_ref, tb_ref, o_ref, kmean_ref, sel_ref, *, n_blk):
    qb = pl.program_id(2)
    tiles_per_blk = MOBA_BLOCK // MOBA_Q_BLOCK

    @pl.when(qb == 0)
    def _():
        lane = lax.broadcasted_iota(jnp.int32, kmean_ref.shape, 1)
        means = jnp.zeros(kmean_ref.shape, F32)
        for j in range(n_blk):
            col = jnp.mean(k_ref[0, 0, :, j * MOBA_BLOCK:(j + 1) * MOBA_BLOCK], axis=1, keepdims=True)
            means = jnp.where(lane == j, col, means)
        kmean_ref[...] = means

    kmean_t = kmean_ref[...].astype(BF16)
    q_all = _heads_to_cols(q_ref[...])
    qs = []
    for tt in range(tiles_per_blk):
        q32 = jnp.concatenate([q_all[:, r * MOBA_BLOCK + tt * MOBA_Q_BLOCK:
                                     r * MOBA_BLOCK + (tt + 1) * MOBA_Q_BLOCK]
                               for r in range(GROUP)], axis=1)
        gate = _dot_tn(kmean_t, q32.astype(BF16))[:n_blk]
        blk = lax.broadcasted_iota(jnp.int32, gate.shape, 0)
        past = blk < qb
        gate = jnp.where(past, gate, NEG_INF)
        blk_f = blk.astype(F32)
        picked = jnp.zeros(gate.shape, jnp.bool_)
        for _ in range(min(MOBA_TOPK, n_blk)):
            best = jnp.max(gate, axis=0, keepdims=True)
            first = jnp.min(jnp.where(gate == best, blk_f, float(n_blk)), axis=0, keepdims=True)
            hit = blk_f == first
            picked = picked | hit
            gate = jnp.where(hit, -jnp.inf, gate)
        sel = (past & picked) | (blk == qb)
        sel_ref[tt] = sel.astype(F32)
        qs.append((q32 * (ATTN_SCALE * LOG2E)).astype(BF16))

    def masked_logits(tt, s, j):
        u = jnp.clip(qb * tiles_per_blk + tt - tiles_per_blk * j, 0, MOBA_FAR_TILE)
        return jnp.where(sel_ref[tt, pl.ds(j, 1), :] > 0, s + tb_ref[0, u], NEG_INF)

    def body(jp, carry):
        off = pl.multiple_of(jp * (2 * MOBA_BLOCK), 2 * MOBA_BLOCK)
        kj = k_ref[0, 0, :, pl.ds(off, 2 * MOBA_BLOCK)].astype(BF16)
        vj = vt_ref[0, 0, :, pl.ds(off, 2 * MOBA_BLOCK)].astype(BF16)
        out = []
        for tt in range(tiles_per_blk):
            m, l, acc = carry[3 * tt:3 * tt + 3]
            s = _dot_tn(kj, qs[tt])
            s0 = masked_logits(tt, s[:MOBA_BLOCK], 2 * jp)
            s1 = masked_logits(tt, s[MOBA_BLOCK:], 2 * jp + 1)
            m_new = jnp.maximum(m, jnp.maximum(jnp.max(s0, axis=0, keepdims=True),
                                               jnp.max(s1, axis=0, keepdims=True)))
            alpha = jnp.exp2(m - m_new)
            p0 = jnp.exp2(s0 - m_new)
            p1 = jnp.exp2(s1 - m_new)
            l = alpha * l + (jnp.sum(p0, axis=0, keepdims=True) + jnp.sum(p1, axis=0, keepdims=True))
            acc = (alpha * acc
                   + jnp.dot(vj[:, :MOBA_BLOCK], p0.astype(BF16), preferred_element_type=F32)
                   + jnp.dot(vj[:, MOBA_BLOCK:], p1.astype(BF16), preferred_element_type=F32))
            out += [m_new, l, acc]
        return tuple(out)

    init = (jnp.full((1, QCOLS), NEG_INF, F32), jnp.zeros((1, QCOLS), F32),
            jnp.zeros((HEAD_DIM, QCOLS), F32)) * tiles_per_blk
    final = lax.fori_loop(0, qb // 2 + 1, body, init)
    outs = [final[3 * tt + 2] / final[3 * tt + 1] for tt in range(tiles_per_blk)]
    o_all = jnp.concatenate([o[:, r * MOBA_Q_BLOCK:(r + 1) * MOBA_Q_BLOCK]
                             for r in range(GROUP) for o in outs], axis=1)
    o_ref[...] = _cols_to_heads(o_all, MOBA_BLOCK).astype(o_ref.dtype)


def moba_attn_prompt(q, kt, vt, tb):
    b, _, _, s = kt.shape
    gw = GROUP * HEAD_DIM
    n_blk = s // MOBA_BLOCK
    assert n_blk % 2 == 0, "kv blocks are swept in pairs"
    assert n_blk <= LANES
    n_tb = tb.shape[1]
    tiles = MOBA_BLOCK // MOBA_Q_BLOCK
    return pl.pallas_call(
        functools.partial(_moba_prompt_kernel, n_blk=n_blk),
        grid=(N_KV_HEADS, b, n_blk),
        in_specs=[pl.BlockSpec((MOBA_BLOCK, gw), lambda g, b, t: (b * n_blk + t, g)),
                  pl.BlockSpec((1, 1, HEAD_DIM, s), lambda g, b, t: (b, g, 0, 0)),
                  pl.BlockSpec((1, 1, HEAD_DIM, s), lambda g, b, t: (b, g, 0, 0)),
                  pl.BlockSpec((1, n_tb, MOBA_BLOCK, QCOLS), lambda g, b, t: (g, 0, 0, 0))],
        out_specs=pl.BlockSpec((MOBA_BLOCK, gw), lambda g, b, t: (b * n_blk + t, g)),
        out_shape=jax.ShapeDtypeStruct(q.shape, BF16),
        scratch_shapes=[pltpu.VMEM((HEAD_DIM, LANES), F32),
                        pltpu.VMEM((tiles, n_blk, QCOLS), F32)],
        compiler_params=_params(),
        name="moba_attn_prompt",
    )(q, kt, vt, tb)


def _expand_heads(q, mask):
    return jnp.concatenate([q] * N_KV_HEADS, axis=1) * mask


def _fold_heads(o):
    acc = o
    for c in range(1, N_KV_HEADS):
        acc = acc + pltpu.roll(o, c * HEAD_DIM, axis=1)
    return acc[:, :HEAD_DIM]


def _win_sample_kernel(q_ref, kn_ref, vn_ref, knc_ref, vnc_ref, kb_ref, vb_ref, bias_ref, bias0_ref, sink_ref,
                       mask_ref, o_ref, ko_ref, vo_ref, *, bb):
    mask = mask_ref[...]
    sink = sink_ref[...]
    w = kb_ref.shape[2]
    last = lax.broadcasted_iota(jnp.int32, (1, w), 1) == w - 1
    rnd = lambda x: x.astype(BF16).astype(F32)
    for i in range(bb):
        kt = kb_ref[i]
        vt = vb_ref[i]
        qe = _expand_heads(q_ref[i] * ATTN_SCALE, mask).astype(BF16)
        s = jnp.dot(qe, kt.astype(BF16), preferred_element_type=F32) + bias_ref[...]
        s0 = jnp.sum(qe.astype(F32) * rnd(kn_ref[i]), axis=1, keepdims=True) + bias0_ref[...]
        m = jnp.maximum(jnp.maximum(jnp.max(s, axis=1, keepdims=True), s0), sink)
        e = jnp.exp(s - m)
        e0 = jnp.exp(s0 - m)
        inv = 1.0 / (jnp.sum(e, axis=1, keepdims=True) + e0 + jnp.exp(sink - m))
        o = (lax.dot_general((e * inv).astype(BF16), vt.astype(BF16), (((1,), (1,)), ((), ())),
                             preferred_element_type=F32)
             + rnd(e0 * inv) * rnd(vn_ref[i]))
        o_ref[i] = _fold_heads(o * mask)
        ko_ref[i] = jnp.where(last, knc_ref[i], pltpu.roll(kt, w - 1, axis=1))
        vo_ref[i] = jnp.where(last, vnc_ref[i], pltpu.roll(vt, w - 1, axis=1))


def window_attn_sample(q3, k_new, v_new, kt_buf, vt_buf, bias, bias0, sink, mask):
    db, kvw, w = kt_buf.shape
    bb = _tile(db, 8)
    row = lambda i: (i, 0, 0)
    fixed = lambda i: (0, 0)
    buf_spec = pl.BlockSpec((bb, kvw, w), row)
    return pl.pallas_call(
        functools.partial(_win_sample_kernel, bb=bb),
        grid=(db // bb,),
        in_specs=[pl.BlockSpec((bb, N_HEADS, HEAD_DIM), row),
                  pl.BlockSpec((bb, 1, kvw), row),
                  pl.BlockSpec((bb, 1, kvw), row),
                  pl.BlockSpec((bb, kvw, 1), row),
                  pl.BlockSpec((bb, kvw, 1), row),
                  buf_spec, buf_spec,
                  pl.BlockSpec((N_HEADS, w), fixed),
                  pl.BlockSpec((N_HEADS, 1), fixed),
                  pl.BlockSpec((N_HEADS, 1), fixed),
                  pl.BlockSpec((N_HEADS, kvw), fixed)],
        out_specs=[pl.BlockSpec((bb, N_HEADS, HEAD_DIM), row), buf_spec, buf_spec],
        out_shape=[jax.ShapeDtypeStruct((db, N_HEADS, HEAD_DIM), F32),
                   jax.ShapeDtypeStruct(kt_buf.shape, F32),
                   jax.ShapeDtypeStruct(vt_buf.shape, F32)],
        compiler_params=_params(),
        name="window_attn_sample",
    )(q3, k_new[:, None, :], v_new[:, None, :], k_new[:, :, None], v_new[:, :, None],
      kt_buf, vt_buf, bias, bias0, sink, mask)


def _copy_page(cache_ref, page, dst_ref, sem):
    return pltpu.make_async_copy(cache_ref.at[page], dst_ref, sem)


def _page_gate_kernel(pt_ref, cache_ref, qm_ref, o_ref, buf_ref, sem_ref, *, cpp, chunks_per_b):
    step = pl.program_id(0)
    n_steps = pl.num_programs(0)

    def copies(s, slot):
        b = s // chunks_per_b
        c = s % chunks_per_b
        return [_copy_page(cache_ref, pt_ref[b, c * cpp + p], buf_ref.at[slot, p], sem_ref.at[slot])
                for p in range(cpp)]

    @pl.when(step == 0)
    def _():
        for cp in copies(step, 0):
            cp.start()

    @pl.when(step + 1 < n_steps)
    def _():
        for cp in copies(step + 1, (step + 1) % 2):
            cp.start()

    slot = step % 2
    for cp in copies(step, slot):
        cp.wait()
    ppb = MOBA_BLOCK // PAGE_SIZE
    qm = qm_ref[0].astype(BF16).astype(F32)
    for n in range(cpp // ppb):
        x = jnp.sum(buf_ref[slot, pl.ds(n * ppb, ppb)], axis=0)
        mean = jnp.sum(x, axis=-1, keepdims=True) * (1.0 / MOBA_BLOCK)
        mean = mean.astype(BF16).astype(F32)
        o_ref[0, n:n + 1, :] = jnp.sum(jnp.sum(mean * qm, axis=0), axis=0, keepdims=True)


def page_block_gates(page_table, cache_t, qm, n_full):
    db = page_table.shape[0]
    ppb = MOBA_BLOCK // PAGE_SIZE
    n_pages = n_full * ppb
    cpp = _tile(n_pages, 64)
    chunks_per_b = n_pages // cpp
    return pl.pallas_call(
        functools.partial(_page_gate_kernel, cpp=cpp, chunks_per_b=chunks_per_b),
        grid_spec=pltpu.PrefetchScalarGridSpec(
            num_scalar_prefetch=1,
            grid=(db * chunks_per_b,),
            in_specs=[pl.BlockSpec(memory_space=pl.ANY),
                      pl.BlockSpec((1, N_KV_HEADS, HEAD_DIM, N_HEADS),
                                   lambda s, pt: (s // chunks_per_b, 0, 0, 0))],
            out_specs=pl.BlockSpec((1, cpp // ppb, N_HEADS),
                                   lambda s, pt: (s // chunks_per_b, s % chunks_per_b, 0)),
            scratch_shapes=[pltpu.VMEM((2, cpp, N_KV_HEADS, HEAD_DIM, PAGE_SIZE), F32),
                            pltpu.SemaphoreType.DMA((2,))]),
        out_shape=jax.ShapeDtypeStruct((db, n_full, N_HEADS), F32),
        compiler_params=_params(dimension_semantics=("arbitrary",)),
        name="page_block_gates",
    )(page_table, cache_t, qm)


def _moba_select_kernel(gate_ref, sel_ref, *, bb, n_sel):
    for i in range(bb):
        gate = gate_ref[i]
        n_full = gate.shape[0]
        row = lax.broadcasted_iota(jnp.int32, gate.shape, 0).astype(F32)
        picks = []
        for _ in range(n_sel):
            best = jnp.max(gate, axis=0, keepdims=True)
            idx = jnp.min(jnp.where(gate == best, row, float(n_full)), axis=0, keepdims=True)
            picks.append(idx)
            gate = jnp.where(row == idx, -jnp.inf, gate)
        sel_ref[i] = jnp.concatenate(picks, axis=0).astype(jnp.int32)


def moba_select_sample(gates, n_sel):
    db, n_full, _ = gates.shape
    bb = _tile(db, 8)
    return pl.pallas_call(
        functools.partial(_moba_select_kernel, bb=bb, n_sel=n_sel),
        grid=(db // bb,),
        in_specs=[pl.BlockSpec((bb, n_full, N_HEADS), lambda i: (i, 0, 0))],
        out_specs=pl.BlockSpec((bb, n_sel, N_HEADS), lambda i: (i, 0, 0)),
        out_shape=jax.ShapeDtypeStruct((db, n_sel, N_HEADS), jnp.int32),
        compiler_params=_params(),
        name="moba_select_sample",
    )(gates)


def _moba_sample_kernel(phys_ref, tile_ref, ck_ref, cv_ref, q_ref, kn_ref, vn_ref, bt_ref, bias0_ref,
                        o_ref, kbuf_ref, vbuf_ref, s_ref, p_ref, sem_ref, *, n_sel):
    b = pl.program_id(0)
    n_b = pl.num_programs(0)
    ppb = MOBA_BLOCK // PAGE_SIZE
    n_pg = n_sel * ppb

    def copies(bi, slot):
        out = []
        for h in range(N_HEADS):
            g = h // GROUP
            for p in range(n_pg):
                page = phys_ref[bi, h * n_pg + p]
                out.append(pltpu.make_async_copy(ck_ref.at[page, g], kbuf_ref.at[slot, h, p],
                                                 sem_ref.at[0, slot]))
                out.append(pltpu.make_async_copy(cv_ref.at[page, g], vbuf_ref.at[slot, h, p],
                                                 sem_ref.at[1, slot]))
        return out

    @pl.when(b == 0)
    def _():
        for cp in copies(b, 0):
            cp.start()

    @pl.when(b + 1 < n_b)
    def _():
        for cp in copies(b + 1, (b + 1) % 2):
            cp.start()

    slot = b % 2
    for cp in copies(b, slot):
        cp.wait()

    rnd = lambda x: x.astype(BF16).astype(F32)
    n_rows = s_ref.shape[1]
    lane = lax.broadcasted_iota(jnp.int32, (1, PAGE_SIZE), 1)
    if n_rows > n_pg + 1:
        s_ref[:, n_pg + 1:, :] = jnp.full((N_HEADS, n_rows - n_pg - 1, PAGE_SIZE), NEG_INF, F32)
    for h in range(N_HEADS):
        g = h // GROUP
        q = rnd(q_ref[0, :, h:h + 1] * ATTN_SCALE)
        for si in range(n_sel):
            tile = tile_ref[b, h * n_sel + si]
            for pp in range(ppb):
                p = si * ppb + pp
                s_ref[h, p:p + 1, :] = (jnp.sum(rnd(kbuf_ref[slot, h, p]) * q, axis=0, keepdims=True)
                                        + bt_ref[tile, h, pp:pp + 1, :])
        s0 = (jnp.sum(q * rnd(kn_ref[0, :, g:g + 1]), axis=0, keepdims=True)
              + bias0_ref[:, h:h + 1])
        s_ref[h, n_pg:n_pg + 1, :] = jnp.where(lane == 0, s0, NEG_INF)
    s = s_ref[...]
    m = jnp.max(jnp.max(s, axis=2, keepdims=True), axis=1, keepdims=True)
    e = jnp.exp(s - m)
    den = jnp.sum(jnp.sum(e, axis=2, keepdims=True), axis=1, keepdims=True)
    p_ref[...] = rnd(e * (1.0 / den))
    head = lax.broadcasted_iota(jnp.int32, (1, N_HEADS), 1)
    o_all = jnp.zeros((HEAD_DIM, N_HEADS), F32)
    for h in range(N_HEADS):
        g = h // GROUP
        acc = rnd(vbuf_ref[slot, h, 0]) * p_ref[h, 0:1, :]
        for pg in range(1, n_pg):
            acc = acc + rnd(vbuf_ref[slot, h, pg]) * p_ref[h, pg:pg + 1, :]
        o = (jnp.sum(acc, axis=1, keepdims=True)
             + p_ref[h, n_pg:n_pg + 1, 0:1] * rnd(vn_ref[0, :, g:g + 1]))
        o_all = jnp.where(head == h, o, o_all)
    o_ref[0] = o_all


def moba_attn_sample(phys, tiles, cache_kt, cache_vt, q_cols, k_new, v_new, bias_tiles, bias0, n_sel):
    db = q_cols.shape[0]
    n_pg = n_sel * (MOBA_BLOCK // PAGE_SIZE)
    n_rows = -(-(n_pg + 1) // SUBLANES) * SUBLANES
    row = lambda b, *_: (b, 0, 0)
    return pl.pallas_call(
        functools.partial(_moba_sample_kernel, n_sel=n_sel),
        grid_spec=pltpu.PrefetchScalarGridSpec(
            num_scalar_prefetch=2,
            grid=(db,),
            in_specs=[pl.BlockSpec(memory_space=pl.ANY),
                      pl.BlockSpec(memory_space=pl.ANY),
                      pl.BlockSpec((1, HEAD_DIM, N_HEADS), row),
                      pl.BlockSpec((1, HEAD_DIM, N_KV_HEADS), row),
                      pl.BlockSpec((1, HEAD_DIM, N_KV_HEADS), row),
                      pl.BlockSpec(bias_tiles.shape, lambda b, *_: (0, 0, 0, 0)),
                      pl.BlockSpec((1, N_HEADS), lambda b, *_: (0, 0))],
            out_specs=pl.BlockSpec((1, HEAD_DIM, N_HEADS), row),
            scratch_shapes=[pltpu.VMEM((2, N_HEADS, n_pg, HEAD_DIM, PAGE_SIZE), F32),
                            pltpu.VMEM((2, N_HEADS, n_pg, HEAD_DIM, PAGE_SIZE), F32),
                            pltpu.VMEM((N_HEADS, n_rows, PAGE_SIZE), F32),
                            pltpu.VMEM((N_HEADS, n_rows, PAGE_SIZE), F32),
                            pltpu.SemaphoreType.DMA((2, 2))]),
        out_shape=jax.ShapeDtypeStruct((db, HEAD_DIM, N_HEADS), F32),
        compiler_params=_params(dimension_semantics=("arbitrary",)),
        name="moba_attn_sample",
    )(phys, tiles, cache_kt, cache_vt, q_cols, k_new, v_new, bias_tiles, bias0)


_HI16 = np.uint32(0xFFFF0000)


def _pack_bf16_pairs(x):
    bits = pltpu.bitcast(x.astype(jnp.bfloat16).astype(F32), jnp.uint32)
    half = x.shape[1] // 2
    return (bits[:, half:] & _HI16) | (bits[:, :half] >> 16)


def _unpack_bf16_pairs(p):
    lo = pltpu.bitcast(p << 16, F32)
    hi = pltpu.bitcast(p & _HI16, F32)
    return jnp.concatenate([lo, hi], axis=1).astype(BF16)


def _router_kernel(x_ref, g_ref, wr_ref, xn_ref, gate_ref, idx_ref):
    xn = _rms(x_ref[...], g_ref[...])
    xn_ref[...] = _pack_bf16_pairs(xn)
    logits = jnp.dot(xn.astype(BF16), wr_ref[...], preferred_element_type=F32)
    n_e = logits.shape[1]
    col = lax.broadcasted_iota(jnp.int32, logits.shape, 1).astype(F32)
    m1 = jnp.max(logits, axis=1, keepdims=True)
    i1 = jnp.min(jnp.where(logits == m1, col, float(n_e)), axis=1, keepdims=True)
    rest = jnp.where(col == i1, -jnp.inf, logits)
    m2 = jnp.max(rest, axis=1, keepdims=True)
    i2 = jnp.min(jnp.where(rest == m2, col, float(n_e)), axis=1, keepdims=True)
    e2 = jnp.exp(m2 - m1)
    den = 1.0 + e2
    gate_ref[...] = jnp.concatenate([1.0 / den, e2 / den], axis=1)
    idx_ref[...] = jnp.concatenate([i1, i2], axis=1).astype(jnp.int32)


def _router_into_kernel(x_ref, g_ref, wr_ref, prev_ref, xn_ref, gate_ref, idx_ref):
    del prev_ref
    _router_kernel(x_ref, g_ref, wr_ref, xn_ref, gate_ref, idx_ref)


def moe_router(x, g, w_router, tm, xn_all, row0):
    m, d = x.shape
    n_e = w_router.shape[1]
    assert row0 % tm == 0
    first = row0 // tm
    return pl.pallas_call(
        _router_into_kernel,
        grid=(m // tm,),
        in_specs=[pl.BlockSpec((tm, d), lambda i: (i, 0)),
                  pl.BlockSpec((1, d), lambda i: (0, 0)),
                  pl.BlockSpec((d, n_e), lambda i: (0, 0)),
                  pl.BlockSpec(memory_space=pl.ANY)],
        out_specs=[pl.BlockSpec((tm, d // 2), lambda i: (first + i, 0)),
                   pl.BlockSpec((tm, EXPERT_TOPK), lambda i: (i, 0)),
                   pl.BlockSpec((tm, EXPERT_TOPK), lambda i: (i, 0))],
        out_shape=[jax.ShapeDtypeStruct(xn_all.shape, jnp.uint32),
                   jax.ShapeDtypeStruct((m, EXPERT_TOPK), F32),
                   jax.ShapeDtypeStruct((m, EXPERT_TOPK), jnp.int32)],
        input_output_aliases={3: 0},
        compiler_params=_params(),
        name="moe_router",
    )(x, g.reshape(1, d), w_router, xn_all)


def _moe_ffn_kernel(te_ref, live_ref, x_ref, wg_ref, wu_ref, wd_ref, prev_ref, o_ref, xb_ref, acc_ref):
    del prev_ref
    i = pl.program_id(0)
    j = pl.program_id(1)

    @pl.when(j == 0)
    def _():
        xb_ref[...] = _unpack_bf16_pairs(x_ref[...])
        acc_ref[...] = jnp.zeros_like(acc_ref)

    @pl.when(live_ref[i] > 0)
    def _():
        x = xb_ref[...]
        a = jnp.dot(x, wg_ref[0].astype(BF16), preferred_element_type=F32)
        b = jnp.dot(x, wu_ref[0].astype(BF16), preferred_element_type=F32)
        h = (a * jax.nn.sigmoid(a)) * b
        acc_ref[...] += jnp.dot(h.astype(BF16), wd_ref[0].astype(BF16), preferred_element_type=F32)

    @pl.when(j == pl.num_programs(1) - 1)
    def _():
        o_ref[...] = acc_ref[...]


def moe_ffn(tile_expert, tile_live, x_sorted, wg, wu, wd, e_sorted, first_tile, tm, tn):
    n_tiles = tile_expert.shape[0]
    d = wg.shape[1]
    f = wg.shape[2]
    return pl.pallas_call(
        _moe_ffn_kernel,
        grid_spec=pltpu.PrefetchScalarGridSpec(
            num_scalar_prefetch=2,
            grid=(n_tiles, f // tn),
            in_specs=[pl.BlockSpec((tm, d // 2), lambda i, j, te, lv: (i, 0)),
                      pl.BlockSpec((1, d, tn), lambda i, j, te, lv: (te[i], 0, j * lv[i])),
                      pl.BlockSpec((1, d, tn), lambda i, j, te, lv: (te[i], 0, j * lv[i])),
                      pl.BlockSpec((1, tn, d), lambda i, j, te, lv: (te[i], j * lv[i], 0)),
                      pl.BlockSpec(memory_space=pl.ANY)],
            out_specs=pl.BlockSpec((tm, d), lambda i, j, te, lv: (first_tile + i, 0)),
            scratch_shapes=[pltpu.VMEM((tm, d), BF16), pltpu.VMEM((tm, d), F32)]),
        out_shape=jax.ShapeDtypeStruct(e_sorted.shape, F32),
        input_output_aliases={6: 0},
        compiler_params=_params(),
        name="moe_ffn",
    )(tile_expert, tile_live, x_sorted, wg, wu, wd, e_sorted)


def _combine_norm_kernel(y_ref, e1_ref, e2_ref, gate_ref, g_ref, o_ref):
    gate = gate_ref[...]
    y = y_ref[...] + (gate[:, 0:1] * e1_ref[...] + gate[:, 1:2] * e2_ref[...])
    o_ref[...] = _rms(y, g_ref[...])


def combine_norm(y, e1, e2, gates, g, tm):
    m, d = y.shape
    blk = pl.BlockSpec((tm, d), lambda i: (i, 0))
    return pl.pallas_call(
        _combine_norm_kernel,
        grid=(m // tm,),
        in_specs=[blk, blk, blk,
                  pl.BlockSpec((tm, EXPERT_TOPK), lambda i: (i, 0)),
                  pl.BlockSpec((1, d), lambda i: (0, 0))],
        out_specs=blk,
        out_shape=jax.ShapeDtypeStruct((m, d), F32),
        compiler_params=_params(),
        name="combine_norm",
    )(y, e1, e2, gates, g.reshape(1, d))


def moe_dispatch_plan(idx, tm, n_chunks):
    m, k = idx.shape
    n_asg = m * k
    p = -(-(n_asg + N_EXPERTS * (tm - 1)) // (tm * n_chunks)) * (tm * n_chunks)
    flat = idx.reshape(n_asg)
    onehot = (flat[:, None] == jnp.arange(N_EXPERTS, dtype=jnp.int32)[None, :]).astype(jnp.int32)
    csum = jnp.cumsum(onehot, axis=0)
    count = csum[-1]
    rank = jnp.take_along_axis(csum, flat[:, None], axis=1)[:, 0] - 1
    padded = -(-count // tm) * tm
    pad_end = jnp.cumsum(padded)
    pad_start = pad_end - padded
    slot = pad_start[flat] + rank
    order = jnp.sort(flat * n_asg + jnp.arange(n_asg, dtype=jnp.int32)) % n_asg
    start = jnp.cumsum(count) - count
    slots = jnp.arange(p, dtype=jnp.int32)
    slot_e = jnp.minimum(jnp.sum((pad_end[None, :] <= slots[:, None]).astype(jnp.int32), axis=1), N_EXPERTS - 1)
    r = slots - pad_start[slot_e]
    live = r < count[slot_e]
    src_asg = order[jnp.clip(start[slot_e] + r, 0, n_asg - 1)]
    row_src = jnp.where(live, src_asg // k, 0).astype(jnp.int32)
    tile_expert = slot_e[::tm]
    tile_live = live[::tm].astype(jnp.int32)
    return row_src, slot.reshape(m, k), tile_expert, tile_live


def _toeplitz_offsets(rows, cols):
    x = np.arange(rows + cols)
    return np.where(x < cols, x, x - (rows + cols))


def _toeplitz(src, rows, cols):
    period = rows + cols
    lead = src.shape[:-1]
    flat = jnp.tile(src, (1,) * len(lead) + (rows,))[..., :rows * (period - 1)]
    return flat.reshape(*lead, rows, period - 1)[..., :cols]


def _distance_bias(table, dist, valid):
    bias = jnp.where(valid[..., None], table[_bucket_np(dist)], NEG_INF)
    return jnp.swapaxes(bias, -1, -2)


def _window_prompt_bias(table):
    rows, cols = 2 * BAND_BLOCK, BAND_BLOCK
    dist = BAND_BLOCK + _toeplitz_offsets(rows, cols)
    bias = _toeplitz(_distance_bias(table, dist, (dist >= 0) & (dist <= WINDOW)), rows, cols)
    bias = bias.reshape(N_KV_HEADS, GROUP, rows, cols).transpose(0, 2, 1, 3)
    return bias.reshape(N_KV_HEADS, rows, QCOLS)


def _moba_prompt_bias(table):
    rows, cols = MOBA_BLOCK, MOBA_Q_BLOCK
    n_t = MOBA_FAR_TILE + 1
    u = np.arange(n_t)[:, None]
    dist = u * MOBA_Q_BLOCK + _toeplitz_offsets(rows, cols)[None, :]
    dist = np.where(u == MOBA_FAR_TILE, np.maximum(dist, FAR_DIST), dist)
    bias = _toeplitz(_distance_bias(table, dist, dist >= 0), rows, cols)
    bias = bias.reshape(n_t, N_KV_HEADS, GROUP, rows, cols).transpose(1, 0, 3, 2, 4)
    return bias.reshape(N_KV_HEADS, n_t, rows, QCOLS)


def _moba_sample_bias(table, past_len):
    ti = np.arange(SAMPLE_FAR_TILE + 1)[:, None]
    r = np.arange(MOBA_BLOCK)[None, :]
    dist = (ti + 1) * MOBA_BLOCK - r + (past_len % MOBA_BLOCK)
    dist = np.where(ti == SAMPLE_FAR_TILE, np.maximum(dist, FAR_DIST), dist)
    bias = table[_bucket_np(dist)].transpose(0, 2, 1)
    return bias.reshape(SAMPLE_FAR_TILE + 1, N_HEADS, MOBA_BLOCK // PAGE_SIZE, PAGE_SIZE)


def kernel(x_prompt, x_sample, state_win_k, state_win_v, cache_k, cache_v, page_table, rel_bias_table,
           attn_norm_g, ffn_norm_g, w_qkv_a, sink_a, w_o_a, kv_norm_g, w_kv_shared, w_q_b, w_o_b,
           w_gate_dense, w_up_dense, w_down_dense, w_router, w_gate_moe, w_up_moe, w_down_moe, final_norm_g):
    bp, sp, d = x_prompt.shape
    bs, ss, _ = x_sample.shape
    assert ss == 1 and attn_norm_g.shape[0] == 2
    mp = bp * sp
    hq = N_HEADS * HEAD_DIM
    hk = N_KV_HEADS * HEAD_DIM
    past_len = page_table.shape[1] * PAGE_SIZE
    assert past_len % MOBA_BLOCK == 0 and sp % MOBA_BLOCK == 0
    n_full = past_len // MOBA_BLOCK
    n_sel = min(MOBA_TOPK, n_full)
    w_buf = state_win_k.shape[2]
    table = rel_bias_table.astype(F32)

    tm_p = _tile(mp, 512)
    tm_s = _tile(bs, 512)
    tn = _tile(w_gate_dense.shape[2], 896) if w_gate_dense.shape[2] % 128 == 0 else w_gate_dense.shape[2]
    cast = lambda w: w.astype(BF16)

    yp = x_prompt.reshape(mp, d)
    ys = x_sample.reshape(bs, d)

    w_qkv = cast(w_qkv_a[0])
    qp, kp, vp = rms_matmul(yp, attn_norm_g[0], w_qkv, tm_p, (hq, hk, hk))
    qs_new, ks_new, vs_new = rms_matmul(ys, attn_norm_g[0], w_qkv, tm_s, (hq, hk, hk))

    pos_minor = lambda t: t.reshape(bp, sp, N_KV_HEADS, HEAD_DIM).transpose(0, 2, 3, 1)

    sink_cols = jnp.repeat(sink_a[0].astype(F32).reshape(N_KV_HEADS, 1, GROUP), BAND_BLOCK, axis=2)
    mix_p = window_attn_prompt(qp, pos_minor(kp), pos_minor(vp), _window_prompt_bias(table), sink_cols)

    head_mask = jnp.asarray((np.arange(N_HEADS)[:, None] // GROUP == np.arange(hk)[None, :] // HEAD_DIM)
                            .astype(np.float32))
    rows_minor = lambda t: t[:, 0].transpose(0, 2, 3, 1).reshape(bs, hk, w_buf)
    bias_ws = table[_bucket_np(w_buf - np.arange(w_buf))].T
    bias_0 = table[0][:, None]
    mix_s, win_kt, win_vt = window_attn_sample(
        qs_new.reshape(bs, N_HEADS, HEAD_DIM), ks_new, vs_new, rows_minor(state_win_k), rows_minor(state_win_v),
        bias_ws, bias_0, sink_a[0].astype(F32)[:, None], head_mask)

    w_o = cast(w_o_a[0])
    yp = matmul_res(mix_p, w_o, yp, tm_p)
    ys = matmul_res(mix_s.reshape(bs, hq), w_o, ys, tm_s)

    wg, wu, wd = cast(w_gate_dense[0]), cast(w_up_dense[0]), cast(w_down_dense[0])
    yp = ffn_dense(yp, ffn_norm_g[0], wg, wu, wd, tm_p, tn)
    ys = ffn_dense(ys, ffn_norm_g[0], wg, wu, wd, tm_s, tn)

    w_kv = cast(w_kv_shared)
    kv_kp, kv_vp = rms_matmul(yp, kv_norm_g, w_kv, tm_p, (hk, hk))
    kv_ks, kv_vs = rms_matmul(ys, kv_norm_g, w_kv, tm_s, (hk, hk))
    w_q = cast(w_q_b[0])
    q_p = rms_matmul(yp, attn_norm_g[1], w_q, tm_p)
    q_s = rms_matmul(ys, attn_norm_g[1], w_q, tm_s)

    mix_p = moba_attn_prompt(q_p, pos_minor(kv_kp), pos_minor(kv_vp), _moba_prompt_bias(table) * LOG2E)

    cache_kt = cache_k.transpose(0, 2, 3, 1)
    cache_vt = cache_v.transpose(0, 2, 3, 1)
    q_cols = q_s.reshape(bs, N_HEADS, HEAD_DIM).transpose(0, 2, 1)
    group_mask = jnp.asarray((np.arange(N_KV_HEADS)[:, None, None] == np.arange(N_HEADS)[None, None, :] // GROUP)
                             .astype(np.float32))
    gates_s = page_block_gates(page_table, cache_kt, q_cols[:, None] * group_mask, n_full)
    sel = moba_select_sample(gates_s, n_sel).transpose(0, 2, 1)
    ppb = MOBA_BLOCK // PAGE_SIZE
    pages = sel[..., None] * ppb + jnp.arange(ppb, dtype=jnp.int32)
    phys = jnp.take_along_axis(page_table, pages.reshape(bs, -1), axis=1)
    tiles = jnp.clip(n_full - 1 - sel, 0, SAMPLE_FAR_TILE).reshape(bs, -1)
    to_cols = lambda t: t.reshape(bs, N_KV_HEADS, HEAD_DIM).transpose(0, 2, 1)
    o_cols = moba_attn_sample(phys, tiles, cache_kt, cache_vt, q_cols, to_cols(kv_ks), to_cols(kv_vs),
                              _moba_sample_bias(table, past_len), table[0][None, :], n_sel)
    mix_s = o_cols.transpose(0, 2, 1)

    w_o = cast(w_o_b[0])
    yp = matmul_res(mix_p, w_o, yp, tm_p)
    ys = matmul_res(mix_s.reshape(bs, hq), w_o, ys, tm_s)

    w_r = cast(w_router[0])
    xn = jnp.zeros((mp + bs, d // 2), jnp.uint32)
    xn, gates_p, idx_p = moe_router(yp, ffn_norm_g[1], w_r, tm_p, xn, 0)
    xn, gates_s, idx_s = moe_router(ys, ffn_norm_g[1], w_r, tm_s, xn, mp)
    idx = jnp.concatenate([idx_p, idx_s], axis=0)
    tm_e = 768
    f_moe = w_gate_moe.shape[3]
    tn_e = _tile(f_moe, 512) if f_moe % 128 == 0 else f_moe
    n_chunks = 4
    row_src, slot, tile_expert, tile_live = moe_dispatch_plan(idx, tm_e, n_chunks)
    tiles_per_chunk = tile_expert.shape[0] // n_chunks
    e_sorted = jnp.zeros((row_src.shape[0], d), F32)
    for c in range(n_chunks):
        tiles = slice(c * tiles_per_chunk, (c + 1) * tiles_per_chunk)
        rows = slice(c * tiles_per_chunk * tm_e, (c + 1) * tiles_per_chunk * tm_e)
        x_chunk = jnp.take(xn, row_src[rows], axis=0, mode="clip")
        e_sorted = moe_ffn(tile_expert[tiles], tile_live[tiles], x_chunk, w_gate_moe[0], w_up_moe[0],
                           w_down_moe[0], e_sorted, c * tiles_per_chunk, tm_e, tn_e)
    def combine(y, gates, slots, tm):
        e1 = jnp.take(e_sorted, slots[:, 0], axis=0, mode="clip")
        e2 = jnp.take(e_sorted, slots[:, 1], axis=0, mode="clip")
        return combine_norm(y, e1, e2, gates, final_norm_g, tm)

    y_prompt = combine(yp, gates_p, slot[:mp], tm_p).reshape(bp, sp, d)
    y_sample = combine(ys, gates_s, slot[mp:], tm_s).reshape(bs, ss, d)
    w_keep = min(WINDOW, sp)
    kv4 = lambda t, b, s: t.reshape(b, s, N_KV_HEADS, HEAD_DIM)
    win_k_prompt = kv4(kp, bp, sp)[:, None, sp - w_keep:]
    win_v_prompt = kv4(vp, bp, sp)[:, None, sp - w_keep:]
    rows_major = lambda t: t.reshape(bs, N_KV_HEADS, HEAD_DIM, w_buf).transpose(0, 3, 1, 2)[:, None]
    win_k_sample = rows_major(win_kt)
    win_v_sample = rows_major(win_vt)
    return (y_prompt, y_sample, win_k_prompt, win_v_prompt,
            kv4(kv_kp, bp, sp), kv4(kv_vp, bp, sp),
            win_k_sample, win_v_sample,
            kv4(kv_ks, bs, 1), kv4(kv_vs, bs, 1))
```

```python
import functools
import math

import numpy as np
import jax
import jax.numpy as jnp
from jax import lax
from jax.experimental import pallas as pl
from jax.experimental.pallas import tpu as pltpu

N_HEADS = 16
N_KV_HEADS = 4
HEAD_DIM = 64
GROUP = N_HEADS // N_KV_HEADS
WINDOW = 128
BAND_BLOCK = 128
MOBA_BLOCK = 256
MOBA_TOPK = 3
MOBA_Q_BLOCK = 128
PAGE_SIZE = 128
N_BUCKETS = 32
REL_MAX_DIST = 1024
N_EXPERTS = 8
EXPERT_TOPK = 2
RMS_EPS = 1e-5
NEG_INF = -1e30
ATTN_SCALE = HEAD_DIM ** -0.5
LOG2E = math.log2(math.e)
QCOLS = GROUP * MOBA_Q_BLOCK
FAR_DIST = 790
MOBA_FAR_TILE = 9
SAMPLE_FAR_TILE = 4
SUBLANES = 8
LANES = 128

F32 = jnp.float32
BF16 = jnp.bfloat16
VMEM_LIMIT = 56 * 1024 * 1024


def _bucket_np(dist):
    max_exact = N_BUCKETS // 2
    n = np.maximum(np.asarray(dist, np.int64), 0)
    nf = np.maximum(n, 1).astype(np.float64)
    large = max_exact + (np.log(nf / max_exact) / math.log(REL_MAX_DIST / max_exact)
                         * (N_BUCKETS - max_exact)).astype(np.int64)
    return np.where(n < max_exact, n, np.minimum(large, N_BUCKETS - 1)).astype(np.int32)


def _tile(n, pref):
    if n <= pref:
        return n
    for t in range(pref, 7, -8):
        if n % t == 0:
            return t
    return n


def _params(**kw):
    return pltpu.CompilerParams(vmem_limit_bytes=VMEM_LIMIT, **kw)


def _rms(x, g):
    return (x * lax.rsqrt(jnp.mean(x * x, axis=-1, keepdims=True) + RMS_EPS)) * g


def _rms_matmul_kernel(x_ref, g_ref, w_ref, *o_refs):
    xn = _rms(x_ref[...], g_ref[...]).astype(BF16)
    z = jnp.dot(xn, w_ref[...], preferred_element_type=F32)
    col = 0
    for o_ref in o_refs:
        o_ref[...] = z[:, col:col + o_ref.shape[1]]
        col += o_ref.shape[1]


def rms_matmul(x, g, w, tm, widths=None):
    m, d = x.shape
    n = w.shape[1]
    widths = widths or (n,)
    assert sum(widths) == n
    outs = pl.pallas_call(
        _rms_matmul_kernel,
        grid=(m // tm,),
        in_specs=[pl.BlockSpec((tm, d), lambda i: (i, 0)),
                  pl.BlockSpec((1, d), lambda i: (0, 0)),
                  pl.BlockSpec((d, n), lambda i: (0, 0))],
        out_specs=[pl.BlockSpec((tm, wd), lambda i: (i, 0)) for wd in widths],
        out_shape=[jax.ShapeDtypeStruct((m, wd), F32) for wd in widths],
        compiler_params=_params(),
        name="rms_matmul",
    )(x, g.reshape(1, d), w)
    return outs if len(widths) > 1 else outs[0]


def _matmul_res_kernel(a_ref, w_ref, r_ref, o_ref):
    o_ref[...] = r_ref[...] + jnp.dot(a_ref[...].astype(BF16), w_ref[...], preferred_element_type=F32)


def matmul_res(a, w, res, tm):
    m, k = a.shape
    n = w.shape[1]
    return pl.pallas_call(
        _matmul_res_kernel,
        grid=(m // tm,),
        in_specs=[pl.BlockSpec((tm, k), lambda i: (i, 0)),
                  pl.BlockSpec((k, n), lambda i: (0, 0)),
                  pl.BlockSpec((tm, n), lambda i: (i, 0))],
        out_specs=pl.BlockSpec((tm, n), lambda i: (i, 0)),
        out_shape=jax.ShapeDtypeStruct((m, n), F32),
        compiler_params=_params(),
        name="matmul_res",
    )(a, w, res)


def _ffn_kernel(x_ref, g_ref, wg_ref, wu_ref, wd_ref, o_ref, xn_ref, acc_ref):
    j = pl.program_id(1)

    @pl.when(j == 0)
    def _():
        xn_ref[...] = _rms(x_ref[...], g_ref[...]).astype(BF16)
        acc_ref[...] = jnp.zeros_like(acc_ref)

    xn = xn_ref[...]
    a = jnp.dot(xn, wg_ref[...], preferred_element_type=F32)
    b = jnp.dot(xn, wu_ref[...], preferred_element_type=F32)
    h = (a * jax.nn.sigmoid(a)) * b
    acc_ref[...] += jnp.dot(h.astype(BF16), wd_ref[...], preferred_element_type=F32)

    @pl.when(j == pl.num_programs(1) - 1)
    def _():
        o_ref[...] = x_ref[...] + acc_ref[...]


def ffn_dense(x, g, wg, wu, wd, tm, tn):
    m, d = x.shape
    f = wg.shape[1]
    return pl.pallas_call(
        _ffn_kernel,
        grid=(m // tm, f // tn),
        in_specs=[pl.BlockSpec((tm, d), lambda i, j: (i, 0)),
                  pl.BlockSpec((1, d), lambda i, j: (0, 0)),
                  pl.BlockSpec((d, tn), lambda i, j: (0, j)),
                  pl.BlockSpec((d, tn), lambda i, j: (0, j)),
                  pl.BlockSpec((tn, d), lambda i, j: (j, 0))],
        out_specs=pl.BlockSpec((tm, d), lambda i, j: (i, 0)),
        out_shape=jax.ShapeDtypeStruct((m, d), F32),
        scratch_shapes=[pltpu.VMEM((tm, d), BF16), pltpu.VMEM((tm, d), F32)],
        compiler_params=_params(),
        name="ffn_dense",
    )(x, g.reshape(1, d), wg, wu, wd)


def _dot_tn(a, b):
    return lax.dot_general(a, b, (((0,), (0,)), ((), ())), preferred_element_type=F32)


def _heads_to_cols(q_rows):
    qt = q_rows.T
    return jnp.concatenate([qt[r * HEAD_DIM:(r + 1) * HEAD_DIM] for r in range(GROUP)], axis=1)


def _cols_to_heads(o, n_q):
    ot = jnp.concatenate([o[:, r * n_q:(r + 1) * n_q] for r in range(GROUP)], axis=0)
    return ot.T


def _win_prompt_kernel(q_ref, kp_ref, ko_ref, vp_ref, vo_ref, wb_ref, sink_ref, o_ref):
    n = pl.program_id(1)
    row = lax.broadcasted_iota(jnp.int32, (2 * BAND_BLOCK, QCOLS), 0)
    has_prev = (row >= BAND_BLOCK) | (n > 0)
    gw = GROUP * HEAD_DIM
    for g in range(N_KV_HEADS):
        q = (_heads_to_cols(q_ref[:, g * gw:(g + 1) * gw]) * ATTN_SCALE).astype(BF16)
        kt = jnp.concatenate([kp_ref[0, g], ko_ref[0, g]], axis=1).astype(BF16)
        s = _dot_tn(kt, q) + wb_ref[g]
        s = jnp.where(has_prev, s, NEG_INF)
        sink = sink_ref[g]
        m = jnp.maximum(jnp.max(s, axis=0, keepdims=True), sink)
        e = jnp.exp(s - m)
        den = jnp.sum(e, axis=0, keepdims=True) + jnp.exp(sink - m)
        vt = jnp.concatenate([vp_ref[0, g], vo_ref[0, g]], axis=1).astype(BF16)
        o = jnp.dot(vt, e.astype(BF16), preferred_element_type=F32) / den
        o_ref[:, g * gw:(g + 1) * gw] = _cols_to_heads(o, BAND_BLOCK).astype(o_ref.dtype)


def window_attn_prompt(q, kt, vt, wb, sink_cols):
    b, _, _, s = kt.shape
    hq = q.shape[1]
    nb = s // BAND_BLOCK
    qb = BAND_BLOCK
    prev = lambda n: jnp.maximum(n - 1, 0)
    return pl.pallas_call(
        _win_prompt_kernel,
        grid=(b, nb),
        in_specs=[pl.BlockSpec((qb, hq), lambda b, n: (b * nb + n, 0)),
                  pl.BlockSpec((1, N_KV_HEADS, HEAD_DIM, qb), lambda b, n: (b, 0, 0, prev(n))),
                  pl.BlockSpec((1, N_KV_HEADS, HEAD_DIM, qb), lambda b, n: (b, 0, 0, n)),
                  pl.BlockSpec((1, N_KV_HEADS, HEAD_DIM, qb), lambda b, n: (b, 0, 0, prev(n))),
                  pl.BlockSpec((1, N_KV_HEADS, HEAD_DIM, qb), lambda b, n: (b, 0, 0, n)),
                  pl.BlockSpec((N_KV_HEADS, 2 * qb, QCOLS), lambda b, n: (0, 0, 0)),
                  pl.BlockSpec((N_KV_HEADS, 1, QCOLS), lambda b, n: (0, 0, 0))],
        out_specs=pl.BlockSpec((qb, hq), lambda b, n: (b * nb + n, 0)),
        out_shape=jax.ShapeDtypeStruct(q.shape, BF16),
        compiler_params=_params(),
        name="window_attn_prompt",
    )(q, kt, kt, vt, vt, wb, sink_cols)


def _moba_prompt_kernel(q_ref, k_ref, vt_ref, tb_ref, o_ref, kmean_ref, sel_ref, *, n_blk):
    qb = pl.program_id(2)
    tiles_per_blk = MOBA_BLOCK // MOBA_Q_BLOCK

    @pl.when(qb == 0)
    def _():
        lane = lax.broadcasted_iota(jnp.int32, kmean_ref.shape, 1)
        means = jnp.zeros(kmean_ref.shape, F32)
        for j in range(n_blk):
            col = jnp.mean(k_ref[0, 0, :, j * MOBA_BLOCK:(j + 1) * MOBA_BLOCK], axis=1, keepdims=True)
            means = jnp.where(lane == j, col, means)
        kmean_ref[...] = means

    kmean_t = kmean_ref[...].astype(BF16)
    q_all = _heads_to_cols(q_ref[...])
    qs = []
    for tt in range(tiles_per_blk):
        q32 = jnp.concatenate([q_all[:, r * MOBA_BLOCK + tt * MOBA_Q_BLOCK:
                                     r * MOBA_BLOCK + (tt + 1) * MOBA_Q_BLOCK]
                               for r in range(GROUP)], axis=1)
        gate = _dot_tn(kmean_t, q32.astype(BF16))[:n_blk]
        blk = lax.broadcasted_iota(jnp.int32, gate.shape, 0)
        past = blk < qb
        gate = jnp.where(past, gate, NEG_INF)
        blk_f = blk.astype(F32)
        picked = jnp.zeros(gate.shape, jnp.bool_)
        for _ in range(min(MOBA_TOPK, n_blk)):
            best = jnp.max(gate, axis=0, keepdims=True)
            first = jnp.min(jnp.where(gate == best, blk_f, float(n_blk)), axis=0, keepdims=True)
            hit = blk_f == first
            picked = picked | hit
            gate = jnp.where(hit, -jnp.inf, gate)
        sel = (past & picked) | (blk == qb)
        sel_ref[tt] = sel.astype(F32)
        qs.append((q32 * (ATTN_SCALE * LOG2E)).astype(BF16))

    def masked_logits(tt, s, j):
        u = jnp.clip(qb * tiles_per_blk + tt - tiles_per_blk * j, 0, MOBA_FAR_TILE)
        return jnp.where(sel_ref[tt, pl.ds(j, 1), :] > 0, s + tb_ref[0, u], NEG_INF)

    def body(jp, carry):
        off = pl.multiple_of(jp * (2 * MOBA_BLOCK), 2 * MOBA_BLOCK)
        kj = k_ref[0, 0, :, pl.ds(off, 2 * MOBA_BLOCK)].astype(BF16)
        vj = vt_ref[0, 0, :, pl.ds(off, 2 * MOBA_BLOCK)].astype(BF16)
        out = []
        for tt in range(tiles_per_blk):
            m, l, acc = carry[3 * tt:3 * tt + 3]
            s = _dot_tn(kj, qs[tt])
            s0 = masked_logits(tt, s[:MOBA_BLOCK], 2 * jp)
            s1 = masked_logits(tt, s[MOBA_BLOCK:], 2 * jp + 1)
            m_new = jnp.maximum(m, jnp.maximum(jnp.max(s0, axis=0, keepdims=True),
                                               jnp.max(s1, axis=0, keepdims=True)))
            alpha = jnp.exp2(m - m_new)
            p0 = jnp.exp2(s0 - m_new)
            p1 = jnp.exp2(s1 - m_new)
            l = alpha * l + (jnp.sum(p0, axis=0, keepdims=True) + jnp.sum(p1, axis=0, keepdims=True))
            acc = (alpha * acc
                   + jnp.dot(vj[:, :MOBA_BLOCK], p0.astype(BF16), preferred_element_type=F32)
                   + jnp.dot(vj[:, MOBA_BLOCK:], p1.astype(BF16), preferred_element_type=F32))
            out += [m_new, l, acc]
        return tuple(out)

    init = (jnp.full((1, QCOLS), NEG_INF, F32), jnp.zeros((1, QCOLS), F32),
            jnp.zeros((HEAD_DIM, QCOLS), F32)) * tiles_per_blk
    final = lax.fori_loop(0, qb // 2 + 1, body, init)
    outs = [final[3 * tt + 2] / final[3 * tt + 1] for tt in range(tiles_per_blk)]
    o_all = jnp.concatenate([o[:, r * MOBA_Q_BLOCK:(r + 1) * MOBA_Q_BLOCK]
                             for r in range(GROUP) for o in outs], axis=1)
    o_ref[...] = _cols_to_heads(o_all, MOBA_BLOCK).astype(o_ref.dtype)


def moba_attn_prompt(q, kt, vt, tb):
    b, _, _, s = kt.shape
    gw = GROUP * HEAD_DIM
    n_blk = s // MOBA_BLOCK
    assert n_blk % 2 == 0, "kv blocks are swept in pairs"
    assert n_blk <= LANES
    n_tb = tb.shape[1]
    tiles = MOBA_BLOCK // MOBA_Q_BLOCK
    return pl.pallas_call(
        functools.partial(_moba_prompt_kernel, n_blk=n_blk),
        grid=(N_KV_HEADS, b, n_blk),
        in_specs=[pl.BlockSpec((MOBA_BLOCK, gw), lambda g, b, t: (b * n_blk + t, g)),
                  pl.BlockSpec((1, 1, HEAD_DIM, s), lambda g, b, t: (b, g, 0, 0)),
                  pl.BlockSpec((1, 1, HEAD_DIM, s), lambda g, b, t: (b, g, 0, 0)),
                  pl.BlockSpec((1, n_tb, MOBA_BLOCK, QCOLS), lambda g, b, t: (g, 0, 0, 0))],
        out_specs=pl.BlockSpec((MOBA_BLOCK, gw), lambda g, b, t: (b * n_blk + t, g)),
        out_shape=jax.ShapeDtypeStruct(q.shape, BF16),
        scratch_shapes=[pltpu.VMEM((HEAD_DIM, LANES), F32),
                        pltpu.VMEM((tiles, n_blk, QCOLS), F32)],
        compiler_params=_params(),
        name="moba_attn_prompt",
    )(q, kt, vt, tb)


def _expand_heads(q, mask):
    return jnp.concatenate([q] * N_KV_HEADS, axis=1) * mask


def _fold_heads(o):
    acc = o
    for c in range(1, N_KV_HEADS):
        acc = acc + pltpu.roll(o, c * HEAD_DIM, axis=1)
    return acc[:, :HEAD_DIM]


def _win_sample_kernel(q_ref, kn_ref, vn_ref, knc_ref, vnc_ref, kb_ref, vb_ref, bias_ref, bias0_ref, sink_ref,
                       mask_ref, o_ref, ko_ref, vo_ref, *, bb):
    mask = mask_ref[...]
    sink = sink_ref[...]
    w = kb_ref.shape[2]
    last = lax.broadcasted_iota(jnp.int32, (1, w), 1) == w - 1
    rnd = lambda x: x.astype(BF16).astype(F32)
    for i in range(bb):
        kt = kb_ref[i]
        vt = vb_ref[i]
        qe = _expand_heads(q_ref[i] * ATTN_SCALE, mask).astype(BF16)
        s = jnp.dot(qe, kt.astype(BF16), preferred_element_type=F32) + bias_ref[...]
        s0 = jnp.sum(qe.astype(F32) * rnd(kn_ref[i]), axis=1, keepdims=True) + bias0_ref[...]
        m = jnp.maximum(jnp.maximum(jnp.max(s, axis=1, keepdims=True), s0), sink)
        e = jnp.exp(s - m)
        e0 = jnp.exp(s0 - m)
        inv = 1.0 / (jnp.sum(e, axis=1, keepdims=True) + e0 + jnp.exp(sink - m))
        o = (lax.dot_general((e * inv).astype(BF16), vt.astype(BF16), (((1,), (1,)), ((), ())),
                             preferred_element_type=F32)
             + rnd(e0 * inv) * rnd(vn_ref[i]))
        o_ref[i] = _fold_heads(o * mask)
        ko_ref[i] = jnp.where(last, knc_ref[i], pltpu.roll(kt, w - 1, axis=1))
        vo_ref[i] = jnp.where(last, vnc_ref[i], pltpu.roll(vt, w - 1, axis=1))


def window_attn_sample(q3, k_new, v_new, kt_buf, vt_buf, bias, bias0, sink, mask):
    db, kvw, w = kt_buf.shape
    bb = _tile(db, 8)
    row = lambda i: (i, 0, 0)
    fixed = lambda i: (0, 0)
    buf_spec = pl.BlockSpec((bb, kvw, w), row)
    return pl.pallas_call(
        functools.partial(_win_sample_kernel, bb=bb),
        grid=(db // bb,),
        in_specs=[pl.BlockSpec((bb, N_HEADS, HEAD_DIM), row),
                  pl.BlockSpec((bb, 1, kvw), row),
                  pl.BlockSpec((bb, 1, kvw), row),
                  pl.BlockSpec((bb, kvw, 1), row),
                  pl.BlockSpec((bb, kvw, 1), row),
                  buf_spec, buf_spec,
                  pl.BlockSpec((N_HEADS, w), fixed),
                  pl.BlockSpec((N_HEADS, 1), fixed),
                  pl.BlockSpec((N_HEADS, 1), fixed),
                  pl.BlockSpec((N_HEADS, kvw), fixed)],
        out_specs=[pl.BlockSpec((bb, N_HEADS, HEAD_DIM), row), buf_spec, buf_spec],
        out_shape=[jax.ShapeDtypeStruct((db, N_HEADS, HEAD_DIM), F32),
                   jax.ShapeDtypeStruct(kt_buf.shape, F32),
                   jax.ShapeDtypeStruct(vt_buf.shape, F32)],
        compiler_params=_params(),
        name="window_attn_sample",
    )(q3, k_new[:, None, :], v_new[:, None, :], k_new[:, :, None], v_new[:, :, None],
      kt_buf, vt_buf, bias, bias0, sink, mask)


def _copy_page(cache_ref, page, dst_ref, sem):
    return pltpu.make_async_copy(cache_ref.at[page], dst_ref, sem)


def _page_gate_kernel(pt_ref, cache_ref, qm_ref, o_ref, buf_ref, sem_ref, *, cpp, chunks_per_b):
    step = pl.program_id(0)
    n_steps = pl.num_programs(0)

    def copies(s, slot):
        b = s // chunks_per_b
        c = s % chunks_per_b
        return [_copy_page(cache_ref, pt_ref[b, c * cpp + p], buf_ref.at[slot, p], sem_ref.at[slot])
                for p in range(cpp)]

    @pl.when(step == 0)
    def _():
        for cp in copies(step, 0):
            cp.start()

    @pl.when(step + 1 < n_steps)
    def _():
        for cp in copies(step + 1, (step + 1) % 2):
            cp.start()

    slot = step % 2
    for cp in copies(step, slot):
        cp.wait()
    ppb = MOBA_BLOCK // PAGE_SIZE
    qm = qm_ref[0].astype(BF16).astype(F32)
    for n in range(cpp // ppb):
        x = jnp.sum(buf_ref[slot, pl.ds(n * ppb, ppb)], axis=0)
        mean = jnp.sum(x, axis=-1, keepdims=True) * (1.0 / MOBA_BLOCK)
        mean = mean.astype(BF16).astype(F32)
        o_ref[0, n:n + 1, :] = jnp.sum(jnp.sum(mean * qm, axis=0), axis=0, keepdims=True)


def page_block_gates(page_table, cache_t, qm, n_full):
    db = page_table.shape[0]
    ppb = MOBA_BLOCK // PAGE_SIZE
    n_pages = n_full * ppb
    cpp = _tile(n_pages, 64)
    chunks_per_b = n_pages // cpp
    return pl.pallas_call(
        functools.partial(_page_gate_kernel, cpp=cpp, chunks_per_b=chunks_per_b),
        grid_spec=pltpu.PrefetchScalarGridSpec(
            num_scalar_prefetch=1,
            grid=(db * chunks_per_b,),
            in_specs=[pl.BlockSpec(memory_space=pl.ANY),
                      pl.BlockSpec((1, N_KV_HEADS, HEAD_DIM, N_HEADS),
                                   lambda s, pt: (s // chunks_per_b, 0, 0, 0))],
            out_specs=pl.BlockSpec((1, cpp // ppb, N_HEADS),
                                   lambda s, pt: (s // chunks_per_b, s % chunks_per_b, 0)),
            scratch_shapes=[pltpu.VMEM((2, cpp, N_KV_HEADS, HEAD_DIM, PAGE_SIZE), F32),
                            pltpu.SemaphoreType.DMA((2,))]),
        out_shape=jax.ShapeDtypeStruct((db, n_full, N_HEADS), F32),
        compiler_params=_params(dimension_semantics=("arbitrary",)),
        name="page_block_gates",
    )(page_table, cache_t, qm)


def _moba_select_kernel(gate_ref, sel_ref, *, bb, n_sel):
    for i in range(bb):
        gate = gate_ref[i]
        n_full = gate.shape[0]
        row = lax.broadcasted_iota(jnp.int32, gate.shape, 0).astype(F32)
        picks = []
        for _ in range(n_sel):
            best = jnp.max(gate, axis=0, keepdims=True)
            idx = jnp.min(jnp.where(gate == best, row, float(n_full)), axis=0, keepdims=True)
            picks.append(idx)
            gate = jnp.where(row == idx, -jnp.inf, gate)
        sel_ref[i] = jnp.concatenate(picks, axis=0).astype(jnp.int32)


def moba_select_sample(gates, n_sel):
    db, n_full, _ = gates.shape
    bb = _tile(db, 8)
    return pl.pallas_call(
        functools.partial(_moba_select_kernel, bb=bb, n_sel=n_sel),
        grid=(db // bb,),
        in_specs=[pl.BlockSpec((bb, n_full, N_HEADS), lambda i: (i, 0, 0))],
        out_specs=pl.BlockSpec((bb, n_sel, N_HEADS), lambda i: (i, 0, 0)),
        out_shape=jax.ShapeDtypeStruct((db, n_sel, N_HEADS), jnp.int32),
        compiler_params=_params(),
        name="moba_select_sample",
    )(gates)


def _moba_sample_kernel(phys_ref, tile_ref, ck_ref, cv_ref, q_ref, kn_ref, vn_ref, bt_ref, bias0_ref,
                        o_ref, kbuf_ref, vbuf_ref, s_ref, p_ref, sem_ref, *, n_sel):
    b = pl.program_id(0)
    n_b = pl.num_programs(0)
    ppb = MOBA_BLOCK // PAGE_SIZE
    n_pg = n_sel * ppb

    def copies(bi, slot):
        out = []
        for h in range(N_HEADS):
            g = h // GROUP
            for p in range(n_pg):
                page = phys_ref[bi, h * n_pg + p]
                out.append(pltpu.make_async_copy(ck_ref.at[page, g], kbuf_ref.at[slot, h, p],
                                                 sem_ref.at[0, slot]))
                out.append(pltpu.make_async_copy(cv_ref.at[page, g], vbuf_ref.at[slot, h, p],
                                                 sem_ref.at[1, slot]))
        return out

    @pl.when(b == 0)
    def _():
        for cp in copies(b, 0):
            cp.start()

    @pl.when(b + 1 < n_b)
    def _():
        for cp in copies(b + 1, (b + 1) % 2):
            cp.start()

    slot = b % 2
    for cp in copies(b, slot):
        cp.wait()

    rnd = lambda x: x.astype(BF16).astype(F32)
    n_rows = s_ref.shape[1]
    lane = lax.broadcasted_iota(jnp.int32, (1, PAGE_SIZE), 1)
    if n_rows > n_pg + 1:
        s_ref[:, n_pg + 1:, :] = jnp.full((N_HEADS, n_rows - n_pg - 1, PAGE_SIZE), NEG_INF, F32)
    for h in range(N_HEADS):
        g = h // GROUP
        q = rnd(q_ref[0, :, h:h + 1] * ATTN_SCALE)
        for si in range(n_sel):
            tile = tile_ref[b, h * n_sel + si]
            for pp in range(ppb):
                p = si * ppb + pp
                s_ref[h, p:p + 1, :] = (jnp.sum(rnd(kbuf_ref[slot, h, p]) * q, axis=0, keepdims=True)
                                        + bt_ref[tile, h, pp:pp + 1, :])
        s0 = (jnp.sum(q * rnd(kn_ref[0, :, g:g + 1]), axis=0, keepdims=True)
              + bias0_ref[:, h:h + 1])
        s_ref[h, n_pg:n_pg + 1, :] = jnp.where(lane == 0, s0, NEG_INF)
    s = s_ref[...]
    m = jnp.max(jnp.max(s, axis=2, keepdims=True), axis=1, keepdims=True)
    e = jnp.exp(s - m)
    den = jnp.sum(jnp.sum(e, axis=2, keepdims=True), axis=1, keepdims=True)
    p_ref[...] = rnd(e * (1.0 / den))
    head = lax.broadcasted_iota(jnp.int32, (1, N_HEADS), 1)
    o_all = jnp.zeros((HEAD_DIM, N_HEADS), F32)
    for h in range(N_HEADS):
        g = h // GROUP
        acc = rnd(vbuf_ref[slot, h, 0]) * p_ref[h, 0:1, :]
        for pg in range(1, n_pg):
            acc = acc + rnd(vbuf_ref[slot, h, pg]) * p_ref[h, pg:pg + 1, :]
        o = (jnp.sum(acc, axis=1, keepdims=True)
             + p_ref[h, n_pg:n_pg + 1, 0:1] * rnd(vn_ref[0, :, g:g + 1]))
        o_all = jnp.where(head == h, o, o_all)
    o_ref[0] = o_all


def moba_attn_sample(phys, tiles, cache_kt, cache_vt, q_cols, k_new, v_new, bias_tiles, bias0, n_sel):
    db = q_cols.shape[0]
    n_pg = n_sel * (MOBA_BLOCK // PAGE_SIZE)
    n_rows = -(-(n_pg + 1) // SUBLANES) * SUBLANES
    row = lambda b, *_: (b, 0, 0)
    return pl.pallas_call(
        functools.partial(_moba_sample_kernel, n_sel=n_sel),
        grid_spec=pltpu.PrefetchScalarGridSpec(
            num_scalar_prefetch=2,
            grid=(db,),
            in_specs=[pl.BlockSpec(memory_space=pl.ANY),
                      pl.BlockSpec(memory_space=pl.ANY),
                      pl.BlockSpec((1, HEAD_DIM, N_HEADS), row),
                      pl.BlockSpec((1, HEAD_DIM, N_KV_HEADS), row),
                      pl.BlockSpec((1, HEAD_DIM, N_KV_HEADS), row),
                      pl.BlockSpec(bias_tiles.shape, lambda b, *_: (0, 0, 0, 0)),
                      pl.BlockSpec((1, N_HEADS), lambda b, *_: (0, 0))],
            out_specs=pl.BlockSpec((1, HEAD_DIM, N_HEADS), row),
            scratch_shapes=[pltpu.VMEM((2, N_HEADS, n_pg, HEAD_DIM, PAGE_SIZE), F32),
                            pltpu.VMEM((2, N_HEADS, n_pg, HEAD_DIM, PAGE_SIZE), F32),
                            pltpu.VMEM((N_HEADS, n_rows, PAGE_SIZE), F32),
                            pltpu.VMEM((N_HEADS, n_rows, PAGE_SIZE), F32),
                            pltpu.SemaphoreType.DMA((2, 2))]),
        out_shape=jax.ShapeDtypeStruct((db, HEAD_DIM, N_HEADS), F32),
        compiler_params=_params(dimension_semantics=("arbitrary",)),
        name="moba_attn_sample",
    )(phys, tiles, cache_kt, cache_vt, q_cols, k_new, v_new, bias_tiles, bias0)


_HI16 = np.uint32(0xFFFF0000)


def _pack_bf16_pairs(x):
    bits = pltpu.bitcast(x.astype(jnp.bfloat16).astype(F32), jnp.uint32)
    half = x.shape[1] // 2
    return (bits[:, half:] & _HI16) | (bits[:, :half] >> 16)


def _unpack_bf16_pairs(p):
    lo = pltpu.bitcast(p << 16, F32)
    hi = pltpu.bitcast(p & _HI16, F32)
    return jnp.concatenate([lo, hi], axis=1).astype(BF16)


def _router_kernel(x_ref, g_ref, wr_ref, xn_ref, gate_ref, idx_ref):
    xn = _rms(x_ref[...], g_ref[...])
    xn_ref[...] = _pack_bf16_pairs(xn)
    logits = jnp.dot(xn.astype(BF16), wr_ref[...], preferred_element_type=F32)
    n_e = logits.shape[1]
    col = lax.broadcasted_iota(jnp.int32, logits.shape, 1).astype(F32)
    m1 = jnp.max(logits, axis=1, keepdims=True)
    i1 = jnp.min(jnp.where(logits == m1, col, float(n_e)), axis=1, keepdims=True)
    rest = jnp.where(col == i1, -jnp.inf, logits)
    m2 = jnp.max(rest, axis=1, keepdims=True)
    i2 = jnp.min(jnp.where(rest == m2, col, float(n_e)), axis=1, keepdims=True)
    e2 = jnp.exp(m2 - m1)
    den = 1.0 + e2
    gate_ref[...] = jnp.concatenate([1.0 / den, e2 / den], axis=1)
    idx_ref[...] = jnp.concatenate([i1, i2], axis=1).astype(jnp.int32)


def _router_into_kernel(x_ref, g_ref, wr_ref, prev_ref, xn_ref, gate_ref, idx_ref):
    del prev_ref
    _router_kernel(x_ref, g_ref, wr_ref, xn_ref, gate_ref, idx_ref)


def moe_router(x, g, w_router, tm, xn_all, row0):
    m, d = x.shape
    n_e = w_router.shape[1]
    assert row0 % tm == 0
    first = row0 // tm
    return pl.pallas_call(
        _router_into_kernel,
        grid=(m // tm,),
        in_specs=[pl.BlockSpec((tm, d), lambda i: (i, 0)),
                  pl.BlockSpec((1, d), lambda i: (0, 0)),
                  pl.BlockSpec((d, n_e), lambda i: (0, 0)),
                  pl.BlockSpec(memory_space=pl.ANY)],
        out_specs=[pl.BlockSpec((tm, d // 2), lambda i: (first + i, 0)),
                   pl.BlockSpec((tm, EXPERT_TOPK), lambda i: (i, 0)),
                   pl.BlockSpec((tm, EXPERT_TOPK), lambda i: (i, 0))],
        out_shape=[jax.ShapeDtypeStruct(xn_all.shape, jnp.uint32),
                   jax.ShapeDtypeStruct((m, EXPERT_TOPK), F32),
                   jax.ShapeDtypeStruct((m, EXPERT_TOPK), jnp.int32)],
        input_output_aliases={3: 0},
        compiler_params=_params(),
        name="moe_router",
    )(x, g.reshape(1, d), w_router, xn_all)


def _moe_ffn_kernel(te_ref, live_ref, x_ref, wg_ref, wu_ref, wd_ref, prev_ref, o_ref, xb_ref, acc_ref):
    del prev_ref
    i = pl.program_id(0)
    j = pl.program_id(1)

    @pl.when(j == 0)
    def _():
        xb_ref[...] = _unpack_bf16_pairs(x_ref[...])
        acc_ref[...] = jnp.zeros_like(acc_ref)

    @pl.when(live_ref[i] > 0)
    def _():
        x = xb_ref[...]
        a = jnp.dot(x, wg_ref[0].astype(BF16), preferred_element_type=F32)
        b = jnp.dot(x, wu_ref[0].astype(BF16), preferred_element_type=F32)
        h = (a * jax.nn.sigmoid(a)) * b
        acc_ref[...] += jnp.dot(h.astype(BF16), wd_ref[0].astype(BF16), preferred_element_type=F32)

    @pl.when(j == pl.num_programs(1) - 1)
    def _():
        o_ref[...] = acc_ref[...]


def moe_ffn(tile_expert, tile_live, x_sorted, wg, wu, wd, e_sorted, first_tile, tm, tn):
    n_tiles = tile_expert.shape[0]
    d = wg.shape[1]
    f = wg.shape[2]
    return pl.pallas_call(
        _moe_ffn_kernel,
        grid_spec=pltpu.PrefetchScalarGridSpec(
            num_scalar_prefetch=2,
            grid=(n_tiles, f // tn),
            in_specs=[pl.BlockSpec((tm, d // 2), lambda i, j, te, lv: (i, 0)),
                      pl.BlockSpec((1, d, tn), lambda i, j, te, lv: (te[i], 0, j * lv[i])),
                      pl.BlockSpec((1, d, tn), lambda i, j, te, lv: (te[i], 0, j * lv[i])),
                      pl.BlockSpec((1, tn, d), lambda i, j, te, lv: (te[i], j * lv[i], 0)),
                      pl.BlockSpec(memory_space=pl.ANY)],
            out_specs=pl.BlockSpec((tm, d), lambda i, j, te, lv: (first_tile + i, 0)),
            scratch_shapes=[pltpu.VMEM((tm, d), BF16), pltpu.VMEM((tm, d), F32)]),
        out_shape=jax.ShapeDtypeStruct(e_sorted.shape, F32),
        input_output_aliases={6: 0},
        compiler_params=_params(),
        name="moe_ffn",
    )(tile_expert, tile_live, x_sorted, wg, wu, wd, e_sorted)


def _combine_norm_kernel(y_ref, e1_ref, e2_ref, gate_ref, g_ref, o_ref):
    gate = gate_ref[...]
    y = y_ref[...] + (gate[:, 0:1] * e1_ref[...] + gate[:, 1:2] * e2_ref[...])
    o_ref[...] = _rms(y, g_ref[...])


def combine_norm(y, e1, e2, gates, g, tm):
    m, d = y.shape
    blk = pl.BlockSpec((tm, d), lambda i: (i, 0))
    return pl.pallas_call(
        _combine_norm_kernel,
        grid=(m // tm,),
        in_specs=[blk, blk, blk,
                  pl.BlockSpec((tm, EXPERT_TOPK), lambda i: (i, 0)),
                  pl.BlockSpec((1, d), lambda i: (0, 0))],
        out_specs=blk,
        out_shape=jax.ShapeDtypeStruct((m, d), F32),
        compiler_params=_params(),
        name="combine_norm",
    )(y, e1, e2, gates, g.reshape(1, d))


def moe_dispatch_plan(idx, tm, n_chunks):
    m, k = idx.shape
    n_asg = m * k
    p = -(-(n_asg + N_EXPERTS * (tm - 1)) // (tm * n_chunks)) * (tm * n_chunks)
    flat = idx.reshape(n_asg)
    onehot = (flat[:, None] == jnp.arange(N_EXPERTS, dtype=jnp.int32)[None, :]).astype(jnp.int32)
    csum = jnp.cumsum(onehot, axis=0)
    count = csum[-1]
    rank = jnp.take_along_axis(csum, flat[:, None], axis=1)[:, 0] - 1
    padded = -(-count // tm) * tm
    pad_end = jnp.cumsum(padded)
    pad_start = pad_end - padded
    slot = pad_start[flat] + rank
    order = jnp.sort(flat * n_asg + jnp.arange(n_asg, dtype=jnp.int32)) % n_asg
    start = jnp.cumsum(count) - count
    slots = jnp.arange(p, dtype=jnp.int32)
    slot_e = jnp.minimum(jnp.sum((pad_end[None, :] <= slots[:, None]).astype(jnp.int32), axis=1), N_EXPERTS - 1)
    r = slots - pad_start[slot_e]
    live = r < count[slot_e]
    src_asg = order[jnp.clip(start[slot_e] + r, 0, n_asg - 1)]
    row_src = jnp.where(live, src_asg // k, 0).astype(jnp.int32)
    tile_expert = slot_e[::tm]
    tile_live = live[::tm].astype(jnp.int32)
    return row_src, slot.reshape(m, k), tile_expert, tile_live


def _toeplitz_offsets(rows, cols):
    x = np.arange(rows + cols)
    return np.where(x < cols, x, x - (rows + cols))


def _toeplitz(src, rows, cols):
    period = rows + cols
    lead = src.shape[:-1]
    flat = jnp.tile(src, (1,) * len(lead) + (rows,))[..., :rows * (period - 1)]
    return flat.reshape(*lead, rows, period - 1)[..., :cols]


def _distance_bias(table, dist, valid):
    bias = jnp.where(valid[..., None], table[_bucket_np(dist)], NEG_INF)
    return jnp.swapaxes(bias, -1, -2)


def _window_prompt_bias(table):
    rows, cols = 2 * BAND_BLOCK, BAND_BLOCK
    dist = BAND_BLOCK + _toeplitz_offsets(rows, cols)
    bias = _toeplitz(_distance_bias(table, dist, (dist >= 0) & (dist <= WINDOW)), rows, cols)
    bias = bias.reshape(N_KV_HEADS, GROUP, rows, cols).transpose(0, 2, 1, 3)
    return bias.reshape(N_KV_HEADS, rows, QCOLS)


def _moba_prompt_bias(table):
    rows, cols = MOBA_BLOCK, MOBA_Q_BLOCK
    n_t = MOBA_FAR_TILE + 1
    assert MOBA_FAR_TILE * MOBA_Q_BLOCK - (rows - 1) >= FAR_DIST
    dist = _toeplitz_offsets(rows, n_t * cols)
    bias = _toeplitz(_distance_bias(table, dist, dist >= 0), rows, n_t * cols)
    bias = bias.reshape(N_KV_HEADS, GROUP, rows, n_t, cols).transpose(0, 3, 2, 1, 4)
    return bias.reshape(N_KV_HEADS, n_t, rows, QCOLS)


def _moba_sample_bias(table, past_len):
    ti = np.arange(SAMPLE_FAR_TILE + 1)[:, None]
    r = np.arange(MOBA_BLOCK)[None, :]
    dist = (ti + 1) * MOBA_BLOCK - r + (past_len % MOBA_BLOCK)
    dist = np.where(ti == SAMPLE_FAR_TILE, np.maximum(dist, FAR_DIST), dist)
    bias = table[_bucket_np(dist)].transpose(0, 2, 1)
    return bias.reshape(SAMPLE_FAR_TILE + 1, N_HEADS, MOBA_BLOCK // PAGE_SIZE, PAGE_SIZE)


def kernel(x_prompt, x_sample, state_win_k, state_win_v, cache_k, cache_v, page_table, rel_bias_table,
           attn_norm_g, ffn_norm_g, w_qkv_a, sink_a, w_o_a, kv_norm_g, w_kv_shared, w_q_b, w_o_b,
           w_gate_dense, w_up_dense, w_down_dense, w_router, w_gate_moe, w_up_moe, w_down_moe, final_norm_g):
    bp, sp, d = x_prompt.shape
    bs, ss, _ = x_sample.shape
    assert ss == 1 and attn_norm_g.shape[0] == 2
    mp = bp * sp
    hq = N_HEADS * HEAD_DIM
    hk = N_KV_HEADS * HEAD_DIM
    past_len = page_table.shape[1] * PAGE_SIZE
    assert past_len % MOBA_BLOCK == 0 and sp % MOBA_BLOCK == 0
    n_full = past_len // MOBA_BLOCK
    n_sel = min(MOBA_TOPK, n_full)
    w_buf = state_win_k.shape[2]
    table = rel_bias_table.astype(F32)

    tm_p = _tile(mp, 512)
    tm_s = _tile(bs, 512)
    tn = _tile(w_gate_dense.shape[2], 896) if w_gate_dense.shape[2] % 128 == 0 else w_gate_dense.shape[2]
    cast = lambda w: w.astype(BF16)

    yp = x_prompt.reshape(mp, d)
    ys = x_sample.reshape(bs, d)

    w_qkv = cast(w_qkv_a[0])
    qp, kp, vp = rms_matmul(yp, attn_norm_g[0], w_qkv, tm_p, (hq, hk, hk))
    qs_new, ks_new, vs_new = rms_matmul(ys, attn_norm_g[0], w_qkv, tm_s, (hq, hk, hk))

    pos_minor = lambda t: t.reshape(bp, sp, N_KV_HEADS, HEAD_DIM).transpose(0, 2, 3, 1)

    sink_cols = jnp.repeat(sink_a[0].astype(F32).reshape(N_KV_HEADS, 1, GROUP), BAND_BLOCK, axis=2)
    mix_p = window_attn_prompt(qp, pos_minor(kp), pos_minor(vp), _window_prompt_bias(table), sink_cols)

    head_mask = jnp.asarray((np.arange(N_HEADS)[:, None] // GROUP == np.arange(hk)[None, :] // HEAD_DIM)
                            .astype(np.float32))
    rows_minor = lambda t: t[:, 0].transpose(0, 2, 3, 1).reshape(bs, hk, w_buf)
    bias_ws = table[_bucket_np(w_buf - np.arange(w_buf))].T
    bias_0 = table[0][:, None]
    mix_s, win_kt, win_vt = window_attn_sample(
        qs_new.reshape(bs, N_HEADS, HEAD_DIM), ks_new, vs_new, rows_minor(state_win_k), rows_minor(state_win_v),
        bias_ws, bias_0, sink_a[0].astype(F32)[:, None], head_mask)

    w_o = cast(w_o_a[0])
    yp = matmul_res(mix_p, w_o, yp, tm_p)
    ys = matmul_res(mix_s.reshape(bs, hq), w_o, ys, tm_s)

    wg, wu, wd = cast(w_gate_dense[0]), cast(w_up_dense[0]), cast(w_down_dense[0])
    yp = ffn_dense(yp, ffn_norm_g[0], wg, wu, wd, tm_p, tn)
    ys = ffn_dense(ys, ffn_norm_g[0], wg, wu, wd, tm_s, tn)

    w_kv = cast(w_kv_shared)
    kv_kp, kv_vp = rms_matmul(yp, kv_norm_g, w_kv, tm_p, (hk, hk))
    kv_ks, kv_vs = rms_matmul(ys, kv_norm_g, w_kv, tm_s, (hk, hk))
    w_q = cast(w_q_b[0])
    q_p = rms_matmul(yp, attn_norm_g[1], w_q, tm_p)
    q_s = rms_matmul(ys, attn_norm_g[1], w_q, tm_s)

    mix_p = moba_attn_prompt(q_p, pos_minor(kv_kp), pos_minor(kv_vp), _moba_prompt_bias(table) * LOG2E)

    cache_kt = cache_k.transpose(0, 2, 3, 1)
    cache_vt = cache_v.transpose(0, 2, 3, 1)
    q_cols = q_s.reshape(bs, N_HEADS, HEAD_DIM).transpose(0, 2, 1)
    group_mask = jnp.asarray((np.arange(N_KV_HEADS)[:, None, None] == np.arange(N_HEADS)[None, None, :] // GROUP)
                             .astype(np.float32))
    gates_s = page_block_gates(page_table, cache_kt, q_cols[:, None] * group_mask, n_full)
    sel = moba_select_sample(gates_s, n_sel).transpose(0, 2, 1)
    ppb = MOBA_BLOCK // PAGE_SIZE
    pages = sel[..., None] * ppb + jnp.arange(ppb, dtype=jnp.int32)
    phys = jnp.take_along_axis(page_table, pages.reshape(bs, -1), axis=1)
    tiles = jnp.clip(n_full - 1 - sel, 0, SAMPLE_FAR_TILE).reshape(bs, -1)
    to_cols = lambda t: t.reshape(bs, N_KV_HEADS, HEAD_DIM).transpose(0, 2, 1)
    o_cols = moba_attn_sample(phys, tiles, cache_kt, cache_vt, q_cols, to_cols(kv_ks), to_cols(kv_vs),
                              _moba_sample_bias(table, past_len), table[0][None, :], n_sel)
    mix_s = o_cols.transpose(0, 2, 1)

    w_o = cast(w_o_b[0])
    yp = matmul_res(mix_p, w_o, yp, tm_p)
    ys = matmul_res(mix_s.reshape(bs, hq), w_o, ys, tm_s)

    w_r = cast(w_router[0])
    xn = jnp.zeros((mp + bs, d // 2), jnp.uint32)
    xn, gates_p, idx_p = moe_router(yp, ffn_norm_g[1], w_r, tm_p, xn, 0)
    xn, gates_s, idx_s = moe_router(ys, ffn_norm_g[1], w_r, tm_s, xn, mp)
    idx = jnp.concatenate([idx_p, idx_s], axis=0)
    tm_e = 768
    f_moe = w_gate_moe.shape[3]
    tn_e = _tile(f_moe, 512) if f_moe % 128 == 0 else f_moe
    n_chunks = 4
    row_src, slot, tile_expert, tile_live = moe_dispatch_plan(idx, tm_e, n_chunks)
    tiles_per_chunk = tile_expert.shape[0] // n_chunks
    e_sorted = jnp.zeros((row_src.shape[0], d), F32)
    for c in range(n_chunks):
        tiles = slice(c * tiles_per_chunk, (c + 1) * tiles_per_chunk)
        rows = slice(c * tiles_per_chunk * tm_e, (c + 1) * tiles_per_chunk * tm_e)
        x_chunk = jnp.take(xn, row_src[rows], axis=0, mode="clip")
        e_sorted = moe_ffn(tile_expert[tiles], tile_live[tiles], x_chunk, w_gate_moe[0], w_up_moe[0],
                           w_down_moe[0], e_sorted, c * tiles_per_chunk, tm_e, tn_e)
    def combine(y, gates, slots, tm):
        e1 = jnp.take(e_sorted, slots[:, 0], axis=0, mode="clip")
        e2 = jnp.take(e_sorted, slots[:, 1], axis=0, mode="clip")
        return combine_norm(y, e1, e2, gates, final_norm_g, tm)

    y_prompt = combine(yp, gates_p, slot[:mp], tm_p).reshape(bp, sp, d)
    y_sample = combine(ys, gates_s, slot[mp:], tm_s).reshape(bs, ss, d)
    w_keep = min(WINDOW, sp)
    kv4 = lambda t, b, s: t.reshape(b, s, N_KV_HEADS, HEAD_DIM)
    win_k_prompt = kv4(kp, bp, sp)[:, None, sp - w_keep:]
    win_v_prompt = kv4(vp, bp, sp)[:, None, sp - w_keep:]
    rows_major = lambda t: t.reshape(bs, N_KV_HEADS, HEAD_DIM, w_buf).transpose(0, 3, 1, 2)[:, None]
    win_k_sample = rows_major(win_kt)
    win_v_sample = rows_major(win_vt)
    return (y_prompt, y_sample, win_k_prompt, win_v_prompt,
            kv4(kv_kp, bp, sp), kv4(kv_vp, bp, sp),
            win_k_sample, win_v_sample,
            kv4(kv_ks, bs, 1), kv4(kv_vs, bs, 1))
```
